```python
import math
import jax, jax.numpy as jnp
from jax import lax
import numpy as np

D_MODEL = 1024
BATCH = 8
SEQ = 4096
DEPTH = 4

N_EVEN = (DEPTH + 1) // 2
N_ODD = DEPTH // 2

ATTN_HEADS = 4
QK_DIM = 64
V_DIM = 2 * QK_DIM
ATTN_WIDTH = ATTN_HEADS * V_DIM
Q_BLOCK = 128

LRU_WIDTH = D_MODEL // 2
LRU_HEADS = 4
LRU_HEAD_DIM = LRU_WIDTH // LRU_HEADS
CONV_WIDTH = 4
LRU_C = 8.0
LRU_MIN_RAD = 0.9
LRU_MAX_RAD = 0.999

Q_COLS = ATTN_HEADS * 2 * QK_DIM
K_COLS = ATTN_HEADS * 2 * QK_DIM
V_COLS = ATTN_WIDTH
IN_COLS = Q_COLS + K_COLS + V_COLS + 2 * LRU_WIDTH
MIX_WIDTH = ATTN_WIDTH + LRU_WIDTH

FOURIER_GROUPS = 4
FOURIER_GROUP_DIM = D_MODEL // FOURIER_GROUPS

D_FF = -(-8 * D_MODEL // (3 * 256)) * 256
EPS = 1e-6

kernel_name = 'hybrid_diffattn_rglru_fnet_encoder'


def rmsnorm(x, g):
    xf = x.astype(jnp.float32)
    y = xf * lax.rsqrt(jnp.mean(xf * xf, axis=-1, keepdims=True) + EPS) * g.astype(jnp.float32)
    return y.astype(x.dtype)


def alibi_slopes(n_heads):
    return jnp.exp2(-8.0 * jnp.arange(1, n_heads + 1, dtype=jnp.float32) / n_heads)


def diff_attention(q, k, v, lam, lam_init, subln_g):
    B, S = q.shape[0], q.shape[1]
    nb = S // Q_BLOCK
    scale = QK_DIM ** -0.5
    slopes = alibi_slopes(ATTN_HEADS)[None, :, None, None, None]
    kpos = jnp.arange(S)
    qb = q.reshape(B, nb, Q_BLOCK, ATTN_HEADS, 2, QK_DIM).transpose(1, 0, 2, 3, 4, 5)

    def block(args):
        qi, bi = args
        s = jnp.einsum('bqhmd,bkhmd->bhmqk', qi, k,
                       preferred_element_type=jnp.float32) * scale
        qpos = bi * Q_BLOCK + jnp.arange(Q_BLOCK)
        dist = jnp.abs(qpos[:, None] - kpos[None, :]).astype(jnp.float32)
        p = jax.nn.softmax(s - slopes * dist, axis=-1)
        a = p[:, :, 0] - lam * p[:, :, 1]
        return jnp.einsum('bhqk,bkhd->bqhd', a.astype(v.dtype), v)

    o = lax.map(block, (qb, jnp.arange(nb)))
    o = o.transpose(1, 0, 2, 3, 4).reshape(B, S, ATTN_HEADS, V_DIM)
    o = rmsnorm(o, subln_g) * (1.0 - lam_init)
    return o.reshape(B, S, ATTN_WIDTH)


def centred_depthwise_conv(x, w, b):
    C = x.shape[-1]
    pad_l = CONV_WIDTH // 2
    pad_r = CONV_WIDTH - 1 - pad_l
    y = lax.conv_general_dilated(x, w[:, None, :], window_strides=(1,),
                                 padding=[(pad_l, pad_r)],
                                 dimension_numbers=('NWC', 'WIO', 'NWC'),
                                 feature_group_count=C)
    return y + b


def _lin_combine(left, right):
    a1, b1 = left
    a2, b2 = right
    return a1 * a2, a2 * b1 + b2


def rg_lru(x, w_a, b_a, w_i, b_i, lam, reverse):
    B, S, C = x.shape
    xh = x.reshape(B, S, LRU_HEADS, LRU_HEAD_DIM)
    r = jax.nn.sigmoid(jnp.einsum('bshi,hij->bshj', xh, w_a).reshape(B, S, C) + b_a)
    i = jax.nn.sigmoid(jnp.einsum('bshi,hij->bshj', xh, w_i).reshape(B, S, C) + b_i)
    log_a = LRU_C * r.astype(jnp.float32) * jax.nn.log_sigmoid(lam.astype(jnp.float32))
    a = jnp.exp(log_a)
    u = jnp.sqrt(-jnp.expm1(2.0 * log_a)) * (i * x).astype(jnp.float32)
    _, h = lax.associative_scan(_lin_combine, (a, u), axis=1, reverse=reverse)
    return h.astype(x.dtype)


def hybrid_mixer(h, w_in, w_out, lq1, lk1, lq2, lk2, subln_g, conv_w, conv_b,
                 wa, ba, wi, bi, lru_lam, lam_init):
    B, S, _ = h.shape
    z = h @ w_in
    c1 = Q_COLS
    c2 = c1 + K_COLS
    c3 = c2 + V_COLS
    c4 = c3 + LRU_WIDTH
    q, k, v, xr, yg = jnp.split(z, [c1, c2, c3, c4], axis=-1)
    q = q.reshape(B, S, ATTN_HEADS, 2, QK_DIM)
    k = k.reshape(B, S, ATTN_HEADS, 2, QK_DIM)
    v = v.reshape(B, S, ATTN_HEADS, V_DIM)
    lam = (jnp.exp(jnp.sum(lq1.astype(jnp.float32) * lk1.astype(jnp.float32)))
           - jnp.exp(jnp.sum(lq2.astype(jnp.float32) * lk2.astype(jnp.float32)))
           + lam_init)
    attn = diff_attention(q, k, v, lam, lam_init, subln_g)
    xc = centred_depthwise_conv(xr, conv_w, conv_b)
    h_fwd = rg_lru(xc, wa[0], ba[0], wi[0], bi[0], lru_lam[0], False)
    h_bwd = rg_lru(xc, wa[1], ba[1], wi[1], bi[1], lru_lam[1], True)
    rec = (h_fwd + h_bwd) * jax.nn.gelu(yg, approximate=True)
    return jnp.concatenate([attn, rec], axis=-1) @ w_out


def fourier_mixer(h, w):
    B, S, D = h.shape
    g = h.astype(jnp.float32).reshape(B, S, FOURIER_GROUPS, FOURIER_GROUP_DIM)
    f = jnp.fft.fft2(g, axes=(1, 3), norm='ortho').real
    return f.reshape(B, S, D).astype(h.dtype) @ w


def swiglu(h, wg, wu, wd):
    return (jax.nn.silu(h @ wg) * (h @ wu)) @ wd


def setup_inputs(seed: int = 0) -> dict:
    key = jax.random.key(seed)
    ks = jax.random.split(key, 24)
    f32 = jnp.float32
    D = D_MODEL
    nrm = lambda k, shape, s: jax.random.normal(k, shape, f32) * s
    gain = lambda k, shape: 1.0 + 0.05 * jax.random.normal(k, shape, f32)
    rad2 = jax.random.uniform(ks[20], (N_EVEN, 2, LRU_WIDTH), f32,
                              LRU_MIN_RAD ** 2, LRU_MAX_RAD ** 2)
    a0 = jnp.sqrt(rad2)
    return {
        'x': jax.random.normal(ks[0], (BATCH, SEQ, D), f32),
        'ln_mix_pre': gain(ks[1], (DEPTH, D)),
        'ln_mix_post': gain(ks[2], (DEPTH, D)),
        'ln_ffn_pre': gain(ks[3], (DEPTH, D)),
        'ln_ffn_post': gain(ks[4], (DEPTH, D)),
        'w_in': nrm(ks[5], (N_EVEN, D, IN_COLS), D ** -0.5),
        'w_mix_out': nrm(ks[6], (N_EVEN, MIX_WIDTH, D), MIX_WIDTH ** -0.5),
        'lambda_q1': nrm(ks[7], (N_EVEN, QK_DIM), 0.1),
        'lambda_k1': nrm(ks[8], (N_EVEN, QK_DIM), 0.1),
        'lambda_q2': nrm(ks[9], (N_EVEN, QK_DIM), 0.1),
        'lambda_k2': nrm(ks[10], (N_EVEN, QK_DIM), 0.1),
        'attn_subln': gain(ks[11], (N_EVEN, V_DIM)),
        'conv_w': nrm(ks[12], (N_EVEN, CONV_WIDTH, LRU_WIDTH), CONV_WIDTH ** -0.5),
        'conv_b': nrm(ks[13], (N_EVEN, LRU_WIDTH), 0.01),
        'lru_w_a': nrm(ks[14], (N_EVEN, 2, LRU_HEADS, LRU_HEAD_DIM, LRU_HEAD_DIM), LRU_HEAD_DIM ** -0.5),
        'lru_b_a': nrm(ks[15], (N_EVEN, 2, LRU_WIDTH), 0.01),
        'lru_w_i': nrm(ks[16], (N_EVEN, 2, LRU_HEADS, LRU_HEAD_DIM, LRU_HEAD_DIM), LRU_HEAD_DIM ** -0.5),
        'lru_b_i': nrm(ks[17], (N_EVEN, 2, LRU_WIDTH), 0.01),
        'lru_lambda': jnp.log(a0) - jnp.log1p(-a0),
        'w_fourier_out': nrm(ks[18], (N_ODD, D, D), D ** -0.5),
        'w_ffn_gate': nrm(ks[19], (DEPTH, D, D_FF), D ** -0.5),
        'w_ffn_up': nrm(ks[21], (DEPTH, D, D_FF), D ** -0.5),
        'w_ffn_down': nrm(ks[22], (DEPTH, D_FF, D), D_FF ** -0.5),
    }


def reference(x, ln_mix_pre, ln_mix_post, ln_ffn_pre, ln_ffn_post, w_in, w_mix_out,
              lambda_q1, lambda_k1, lambda_q2, lambda_k2, attn_subln, conv_w, conv_b,
              lru_w_a, lru_b_a, lru_w_i, lru_b_i, lru_lambda, w_fourier_out,
              w_ffn_gate, w_ffn_up, w_ffn_down):
    for l in range(DEPTH):
        hn = rmsnorm(x, ln_mix_pre[l])
        if l % 2 == 0:
            e = l // 2
            lam_init = 0.8 - 0.6 * math.exp(-0.3 * l)
            m = hybrid_mixer(hn, w_in[e], w_mix_out[e], lambda_q1[e], lambda_k1[e],
                             lambda_q2[e], lambda_k2[e], attn_subln[e], conv_w[e], conv_b[e],
                             lru_w_a[e], lru_b_a[e], lru_w_i[e], lru_b_i[e], lru_lambda[e],
                             lam_init)
        else:
            m = fourier_mixer(hn, w_fourier_out[l // 2])
        x = x + rmsnorm(m, ln_mix_post[l])
        hn = rmsnorm(x, ln_ffn_pre[l])
        x = x + rmsnorm(swiglu(hn, w_ffn_gate[l], w_ffn_up[l], w_ffn_down[l]), ln_ffn_post[l])
    return x
```

```python
import functools

import numpy as np
import jax
import jax.numpy as jnp
from jax import lax
from jax.experimental import pallas as pl
from jax.experimental.pallas import tpu as pltpu

F32 = jnp.float32
BF16 = jnp.bfloat16

EPS = 1e-6
ATTN_HEADS = 4
QK_DIM = 64
HEAD_W = 2 * QK_DIM
LRU_HEADS = 4
LRU_HEAD_DIM = 128
CONV_WIDTH = 4
LRU_C = 8.0
FOURIER_GROUPS = 4
FOURIER_RADIX = 4

VMEM_LIMIT_BYTES = 54 * 1024 * 1024
POS_SPLIT = 64


def _rms(x, g):
    ms = jnp.mean(x * x, axis=-1, keepdims=True)
    return x * lax.rsqrt(ms + EPS) * g


def _sigmoid(x):
    return 1.0 / (1.0 + jnp.exp(-x))


def _params(*semantics):
    return pltpu.CompilerParams(dimension_semantics=semantics, vmem_limit_bytes=VMEM_LIMIT_BYTES)


def _resident(shape):
    nd = len(shape)
    return pl.BlockSpec(shape, lambda *_: (0,) * nd, pipeline_mode=pl.Buffered(1))


def _inproj_kernel(x_ref, g_ref, w_ref, q_ref, k_ref, v_ref, xr_ref, yg_ref):
    h = _rms(x_ref[...], g_ref[...]).astype(BF16)
    width = q_ref.shape[1]
    for n, o_ref in enumerate((q_ref, k_ref, v_ref, xr_ref, yg_ref)):
        z = jnp.dot(h, w_ref[:, n * width:(n + 1) * width], preferred_element_type=F32)
        if n == 0:
            z = z * (QK_DIM ** -0.5)
        o_ref[...] = z.astype(o_ref.dtype)


def _inproj(x, g, w, tm=512):
    T, D = x.shape
    width = w.shape[1] // 5
    row = lambda i: (i, 0)
    out_shapes = [jax.ShapeDtypeStruct((T, width), dt) for dt in (BF16, BF16, BF16, F32, F32)]
    return pl.pallas_call(
        _inproj_kernel,
        grid=(T // tm,),
        in_specs=[pl.BlockSpec((tm, D), row), _resident((1, D)), _resident(w.shape)],
        out_specs=[pl.BlockSpec((tm, width), row) for _ in out_shapes],
        out_shape=out_shapes,
        compiler_params=_params("parallel"),
        name="inproj",
    )(x, g.reshape(1, D), w)


def _attn_kernel(lam_init, tq, tk, q_ref, k_ref, v_ref, slope_ref, lq1_ref, lk1_ref, lq2_ref, lk2_ref,
                 g_ref, o_ref, kt_ref, va_ref, s_ref, p_ref):
    S = k_ref.shape[1]
    nchunk = S // tk
    per_q = tq // tk
    i = pl.program_id(2)
    sig_row = slope_ref[0]
    sig = sig_row[:, 0:1]

    @pl.when(i == 0)
    def _build_keys_values():
        ch = 512
        for c in range(S // ch):
            cols = slice(c * ch, (c + 1) * ch)
            kt = k_ref[0, cols, :].astype(F32).T
            sub = lax.broadcasted_iota(jnp.int32, (HEAD_W, ch), 0)
            j = lax.broadcasted_iota(jnp.int32, (HEAD_W, ch), 1) + c * ch
            j_lo = j & (POS_SPLIT - 1)
            j_hi = j - j_lo
            aug = jnp.where(sub < 2, 1.0,
                            jnp.where(sub == 2, sig * j_lo.astype(F32),
                                      jnp.where(sub == 3, sig * j_hi.astype(F32), 0.0)))
            kt_ref[0, 0:HEAD_W, cols] = jnp.where(sub < QK_DIM, kt, 0.0).astype(BF16)
            kt_ref[1, 0:HEAD_W, cols] = jnp.where(sub >= QK_DIM, kt, 0.0).astype(BF16)
            kt_ref[0, HEAD_W:2 * HEAD_W, cols] = aug.astype(BF16)
            kt_ref[1, HEAD_W:2 * HEAD_W, cols] = aug.astype(BF16)
        lane = lax.broadcasted_iota(jnp.int32, (S, HEAD_W), 1)
        va_ref[:, 0:HEAD_W] = v_ref[0]
        va_ref[:, HEAD_W:2 * HEAD_W] = jnp.where(lane == 0, 1.0, 0.0).astype(BF16)

    i0 = i * tq
    q = q_ref[0]
    lane = lax.broadcasted_iota(jnp.int32, (tq, HEAD_W), 1)
    row = lax.broadcasted_iota(jnp.int32, (tq, HEAD_W), 0) + i0
    r_lo = row & (POS_SPLIT - 1)
    r_hi = row - r_lo
    aug_left = jnp.where(lane == 0, -sig_row * r_lo.astype(F32),
                         jnp.where(lane == 1, -sig_row * r_hi.astype(F32),
                                   jnp.where(lane < 4, 1.0, 0.0)))
    q_left = jnp.concatenate([q, aug_left.astype(BF16)], axis=1)
    q_right = jnp.concatenate([q, (-aug_left).astype(BF16)], axis=1)

    lam = (jnp.exp(jnp.sum(lq1_ref[...] * lk1_ref[...], axis=1, keepdims=True))
           - jnp.exp(jnp.sum(lq2_ref[...] * lk2_ref[...], axis=1, keepdims=True)) + lam_init)

    a_idx = lax.broadcasted_iota(jnp.int32, (tq, tk), 0)
    b_idx = lax.broadcasted_iota(jnp.int32, (tq, tk), 1)
    outs = []
    for m in range(2):
        for c in range(nchunk):
            is_right = c * tk >= i0 + tq
            qs = jnp.where(is_right, q_right, q_left)
            s_ref[c] = jnp.dot(qs, kt_ref[m, :, c * tk:(c + 1) * tk], preferred_element_type=F32)
        for d in range(per_q):
            rel = b_idx - a_idx + d * tk
            corr = (-2.0 * sig) * jnp.maximum(rel, 0).astype(F32)
            s_ref[i * per_q + d] = s_ref[i * per_q + d] + corr
        mx = s_ref[0]
        for c in range(1, nchunk):
            mx = jnp.maximum(mx, s_ref[c])
        mrow = jnp.max(mx, axis=1, keepdims=True)
        for c in range(nchunk):
            p_ref[:, c * tk:(c + 1) * tk] = jnp.exp(s_ref[c] - mrow).astype(BF16)
        oa = jnp.dot(p_ref[...], va_ref[...], preferred_element_type=F32)
        outs.append(oa[:, 0:HEAD_W] / oa[:, HEAD_W:HEAD_W + 1])
    o = outs[0] - lam * outs[1]
    o = _rms(o, g_ref[...]) * (1.0 - lam_init)
    o_ref[0] = o.astype(o_ref.dtype)


def _attention(q, k, v, slopes, lq1, lk1, lq2, lk2, subln_g, lam_init, tq=256, tk=256):
    B, S, W = q.shape
    H = W // HEAD_W
    kernel = functools.partial(_attn_kernel, lam_init, tq, tk)
    vec = lambda a: a.reshape(1, -1).astype(F32)
    small = lambda n: pl.BlockSpec((1, n), lambda b, h, i: (0, 0))
    return pl.pallas_call(
        kernel,
        grid=(B, H, S // tq),
        in_specs=[
            pl.BlockSpec((1, tq, HEAD_W), lambda b, h, i: (b, i, h)),
            pl.BlockSpec((1, S, HEAD_W), lambda b, h, i: (b, 0, h)),
            pl.BlockSpec((1, S, HEAD_W), lambda b, h, i: (b, 0, h)),
            pl.BlockSpec((1, 1, HEAD_W), lambda b, h, i: (h, 0, 0)),
            small(QK_DIM), small(QK_DIM), small(QK_DIM), small(QK_DIM), small(HEAD_W),
        ],
        out_specs=pl.BlockSpec((1, tq, HEAD_W), lambda b, h, i: (b, i, h)),
        out_shape=jax.ShapeDtypeStruct((B, S, W), BF16),
        scratch_shapes=[
            pltpu.VMEM((2, 2 * HEAD_W, S), BF16),
            pltpu.VMEM((S, 2 * HEAD_W), BF16),
            pltpu.VMEM((S // tk, tq, tk), F32),
            pltpu.VMEM((tq, S), BF16),
        ],
        compiler_params=_params("parallel", "parallel", "arbitrary"),
        name="diff_attention",
    )(q, k, v, slopes, vec(lq1), vec(lk1), vec(lq2), vec(lk2), vec(subln_g))


def _scan8(a, b, reverse):
    row = lax.broadcasted_iota(jnp.int32, a.shape, 0)
    for k in (1, 2, 4):
        if reverse:
            keep = row < 8 - k
            shift = 8 - k
        else:
            keep = row >= k
            shift = k
        a_s = jnp.where(keep, pltpu.roll(a, shift, 0), 1.0)
        b_s = jnp.where(keep, pltpu.roll(b, shift, 0), 0.0)
        b = a * b_s + b
        a = a * a_s
    return a, b


def _lru_kernel(xr_ref, yg_ref, cw_ref, cb_ref, wg_ref, bg_ref, lam_ref, o_ref,
                xp_ref, af_ref, uf_ref, ab_ref, ub_ref):
    S = xr_ref.shape[1]
    C = LRU_HEAD_DIM
    tc = 512
    pad = 8
    xp_ref[0:pad, :] = jnp.zeros((pad, C), F32)
    xp_ref[S + pad:S + 2 * pad, :] = jnp.zeros((pad, C), F32)
    xp_ref[pad:S + pad, :] = xr_ref[0]

    lam = lam_ref[0]
    log_sig = jnp.minimum(lam, 0.0) - jnp.log1p(jnp.exp(-jnp.abs(lam)))
    cw = cw_ref[...]
    for c in range(S // tc):
        base = c * tc
        xc = cb_ref[...]
        for t in range(CONV_WIDTH):
            off = base + pad - CONV_WIDTH // 2 + t
            xc = xc + cw[t:t + 1, :] * xp_ref[off:off + tc, :]
        gates = jnp.dot(xc.astype(BF16), wg_ref[0], preferred_element_type=F32) + bg_ref[0]
        for d, (a_ref, u_ref) in enumerate(((af_ref, uf_ref), (ab_ref, ub_ref))):
            r = _sigmoid(gates[:, (2 * d) * C:(2 * d + 1) * C])
            ig = _sigmoid(gates[:, (2 * d + 1) * C:(2 * d + 2) * C])
            log_a = LRU_C * r * log_sig[:, d * C:(d + 1) * C]
            a = jnp.exp(log_a)
            one_minus_a2 = -jnp.tanh(log_a) * (1.0 + a * a)
            a_ref[base:base + tc, :] = a
            u_ref[base:base + tc, :] = jnp.sqrt(one_minus_a2) * (ig * xc)

    unroll = 8
    span = 8 * unroll

    def body(it, carry):
        cf, cb = carry
        f0 = pl.multiple_of(it * span, span)
        b0 = pl.multiple_of(S - (it + 1) * span, span)
        af = af_ref[pl.ds(f0, span), :]
        uf = uf_ref[pl.ds(f0, span), :]
        hs = []
        for j in range(unroll):
            a, b = _scan8(af[8 * j:8 * j + 8], uf[8 * j:8 * j + 8], False)
            hs.append(a * cf + b)
            cf = jnp.broadcast_to(a[7:8], (8, C)) * cf + jnp.broadcast_to(b[7:8], (8, C))
        uf_ref[pl.ds(f0, span), :] = jnp.concatenate(hs, axis=0)
        ab = ab_ref[pl.ds(b0, span), :]
        ub = ub_ref[pl.ds(b0, span), :]
        hs = []
        for j in reversed(range(unroll)):
            a, b = _scan8(ab[8 * j:8 * j + 8], ub[8 * j:8 * j + 8], True)
            hs.append(a * cb + b)
            cb = jnp.broadcast_to(a[0:1], (8, C)) * cb + jnp.broadcast_to(b[0:1], (8, C))
        ub_ref[pl.ds(b0, span), :] = jnp.concatenate(hs[::-1], axis=0)
        return cf, cb

    zero = jnp.zeros((8, C), F32)
    lax.fori_loop(0, S // span, body, (zero, zero))

    for c in range(S // tc):
        rows = slice(c * tc, (c + 1) * tc)
        y = yg_ref[0, rows, :]
        gelu = 0.5 * y * (1.0 + jnp.tanh(0.7978845608028654 * (y + 0.044715 * (y * y * y))))
        o_ref[0, rows, :] = ((uf_ref[rows, :] + ub_ref[rows, :]) * gelu).astype(o_ref.dtype)


def _lru(xr, yg, conv_w, conv_b, w_a, b_a, w_i, b_i, lam):
    B, S, W = xr.shape
    C = LRU_HEAD_DIM
    H = W // C
    wg = jnp.concatenate([w_a[0], w_i[0], w_a[1], w_i[1]], axis=-1).astype(BF16)
    per_head = lambda a: a.reshape(H, 1, C)
    bg = jnp.concatenate([per_head(b_a[0]), per_head(b_i[0]), per_head(b_a[1]), per_head(b_i[1])], axis=-1)
    lam2 = jnp.concatenate([per_head(lam[0]), per_head(lam[1])], axis=-1)
    seq = pl.BlockSpec((1, S, C), lambda b, h: (b, 0, h))
    return pl.pallas_call(
        _lru_kernel,
        grid=(B, H),
        in_specs=[
            seq, seq,
            pl.BlockSpec((CONV_WIDTH, C), lambda b, h: (0, h)),
            pl.BlockSpec((1, C), lambda b, h: (0, h)),
            pl.BlockSpec((1, C, 4 * C), lambda b, h: (h, 0, 0)),
            pl.BlockSpec((1, 1, 4 * C), lambda b, h: (h, 0, 0)),
            pl.BlockSpec((1, 1, 2 * C), lambda b, h: (h, 0, 0)),
        ],
        out_specs=seq,
        out_shape=jax.ShapeDtypeStruct((B, S, W), BF16),
        scratch_shapes=[pltpu.VMEM((S + 16, C), F32)] + [pltpu.VMEM((S, C), F32) for _ in range(4)],
        compiler_params=_params("parallel", "parallel"),
        name="rg_lru",
    )(xr, yg, conv_w, conv_b.reshape(1, W), wg, bg, lam2)


def _prenorm_kernel(x_ref, g_ref, o_ref):
    o_ref[...] = _rms(x_ref[...], g_ref[...]).astype(o_ref.dtype)


def _prenorm(x, g, tm=1024):
    T, D = x.shape
    row = lambda i: (i, 0)
    return pl.pallas_call(
        _prenorm_kernel,
        grid=(T // tm,),
        in_specs=[pl.BlockSpec((tm, D), row), _resident((1, D))],
        out_specs=pl.BlockSpec((tm, D), row),
        out_shape=jax.ShapeDtypeStruct((T, D), BF16),
        compiler_params=_params("parallel"),
        name="prenorm",
    )(x, g.reshape(1, D))


@functools.lru_cache(maxsize=None)
def _dft_tables(seq, group_dim):
    n4 = seq // FOURIER_RADIX
    k1 = np.arange(n4, dtype=np.int64)[:, None]
    s1 = np.arange(n4, dtype=np.int64)[None, :]
    cos_t, sin_t = [], []
    for s2 in range(FOURIER_RADIX):
        ang = 2.0 * np.pi * ((k1 * (FOURIER_RADIX * s1 + s2)) % seq).astype(np.float64) / seq
        cos_t.append(np.cos(ang) / np.sqrt(seq))
        sin_t.append(-np.sin(ang) / np.sqrt(seq))
    c = np.arange(group_dim, dtype=np.int64)
    ang = 2.0 * np.pi * ((c[:, None] * c[None, :]) % group_dim).astype(np.float64) / group_dim
    chan = np.concatenate([np.cos(ang), np.sin(ang)], axis=0) / np.sqrt(group_dim)
    return (np.stack(cos_t).astype(np.float32), np.stack(sin_t).astype(np.float32), chan.astype(np.float32))


def _fourier_kernel(h0_ref, h1_ref, h2_ref, h3_ref, fc_ref, fs_ref, cs_ref, o_ref):
    n4 = fc_ref.shape[1]
    rows = 256
    for kc in range(n4 // rows):
        rs = slice(kc * rows, (kc + 1) * rows)
        yr, yi = [], []
        for r, h_ref in enumerate((h0_ref, h1_ref, h2_ref, h3_ref)):
            x = h_ref[0]
            yr.append(jnp.dot(fc_ref[r, rs, :], x, preferred_element_type=F32))
            yi.append(jnp.dot(fs_ref[r, rs, :], x, preferred_element_type=F32))
        xs = (
            (yr[0] + yr[1] + yr[2] + yr[3], yi[0] + yi[1] + yi[2] + yi[3]),
            (yr[0] + yi[1] - yr[2] - yi[3], yi[0] - yr[1] - yi[2] + yr[3]),
            (yr[0] - yr[1] + yr[2] - yr[3], yi[0] - yi[1] + yi[2] - yi[3]),
            (yr[0] - yi[1] - yr[2] + yi[3], yi[0] + yr[1] - yi[2] - yr[3]),
        )
        for k2, (xr, xi) in enumerate(xs):
            z = jnp.concatenate([xr.astype(BF16), xi.astype(BF16)], axis=1)
            f = jnp.dot(z, cs_ref[...], preferred_element_type=F32)
            o_ref[0, k2 * n4 + kc * rows:k2 * n4 + (kc + 1) * rows, :] = f.astype(o_ref.dtype)


def _fourier(hn, B, S):
    T, D = hn.shape
    G = D // FOURIER_GROUPS
    n4 = S // FOURIER_RADIX
    fc, fs, cs = (jnp.asarray(t).astype(BF16) for t in _dft_tables(S, G))
    h4 = hn.reshape(B, n4, FOURIER_RADIX * D)
    per_res = lambda r: pl.BlockSpec((1, n4, G), lambda b, g: (b, 0, r * FOURIER_GROUPS + g))
    return pl.pallas_call(
        _fourier_kernel,
        grid=(B, FOURIER_GROUPS),
        in_specs=[per_res(0), per_res(1), per_res(2), per_res(3),
                  _resident(fc.shape), _resident(fs.shape), _resident(cs.shape)],
        out_specs=pl.BlockSpec((1, S, G), lambda b, g: (b, 0, g)),
        out_shape=jax.ShapeDtypeStruct((B, S, D), BF16),
        compiler_params=_params("parallel", "parallel"),
        name="fourier",
    )(h4, h4, h4, h4, fc, fs, cs)


def _post_ffn_kernel(x_ref, a_ref, b_ref, wa_ref, wb_ref, gpost_ref, gpre_ref, wg_ref, wu_ref, wd_ref,
                     gfpost_ref, o_ref, act_ref):
    m = (jnp.dot(a_ref[...], wa_ref[...], preferred_element_type=F32)
         + jnp.dot(b_ref[...], wb_ref[...], preferred_element_type=F32))
    x = x_ref[...] + _rms(m, gpost_ref[...])
    h = _rms(x, gpre_ref[...]).astype(BF16)
    dff = wg_ref.shape[1]
    ch = 256
    for c in range(dff // ch):
        cols = slice(c * ch, (c + 1) * ch)
        g = jnp.dot(h, wg_ref[:, cols], preferred_element_type=F32)
        u = jnp.dot(h, wu_ref[:, cols], preferred_element_type=F32)
        act_ref[:, cols] = (g * _sigmoid(g) * u).astype(BF16)
    y = jnp.dot(act_ref[...], wd_ref[...], preferred_element_type=F32)
    o_ref[...] = x + _rms(y, gfpost_ref[...])


def _post_ffn(x, a, a_col, b, b_col, w_mix, g_post, g_pre, wg, wu, wd, g_fpost, tm=512):
    T, D = x.shape
    half = w_mix.shape[0] // 2
    dff = wg.shape[1]
    row = lambda i: (i, 0)
    vec = lambda g: g.reshape(1, D)
    return pl.pallas_call(
        _post_ffn_kernel,
        grid=(T // tm,),
        in_specs=[
            pl.BlockSpec((tm, D), row),
            pl.BlockSpec((tm, half), lambda i: (i, a_col)),
            pl.BlockSpec((tm, half), lambda i: (i, b_col)),
            pl.BlockSpec((half, D), lambda i: (0, 0), pipeline_mode=pl.Buffered(1)),
            pl.BlockSpec((half, D), lambda i: (1, 0), pipeline_mode=pl.Buffered(1)),
            _resident((1, D)), _resident((1, D)),
            _resident(wg.shape), _resident(wu.shape), _resident(wd.shape),
            _resident((1, D)),
        ],
        out_specs=pl.BlockSpec((tm, D), row),
        out_shape=jax.ShapeDtypeStruct((T, D), F32),
        scratch_shapes=[pltpu.VMEM((tm, dff), BF16)],
        compiler_params=_params("parallel"),
        name="post_ffn",
    )(x, a, b, w_mix, w_mix, vec(g_post), vec(g_pre), wg, wu, wd, vec(g_fpost))


def kernel(x, ln_mix_pre, ln_mix_post, ln_ffn_pre, ln_ffn_post, w_in, w_mix_out, lambda_q1, lambda_k1,
           lambda_q2, lambda_k2, attn_subln, conv_w, conv_b, lru_w_a, lru_b_a, lru_w_i, lru_b_i, lru_lambda,
           w_fourier_out, w_ffn_gate, w_ffn_up, w_ffn_down):
    B, S, D = x.shape
    T = B * S
    depth = ln_mix_pre.shape[0]
    heads = jnp.arange(1, ATTN_HEADS + 1, dtype=F32)
    slopes = jnp.broadcast_to(jnp.exp2(-8.0 * heads / ATTN_HEADS)[:, None, None], (ATTN_HEADS, 1, HEAD_W))
    xf = x.reshape(T, D)
    for l in range(depth):
        if l % 2 == 0:
            e = l // 2
            lam_init = 0.8 - 0.6 * float(np.exp(-0.3 * l))
            q, k, v, xr, yg = _inproj(xf, ln_mix_pre[l], w_in[e].astype(BF16))
            width = q.shape[1]
            to_seq = lambda a: a.reshape(B, S, width)
            attn = _attention(to_seq(q), to_seq(k), to_seq(v), slopes, lambda_q1[e], lambda_k1[e],
                              lambda_q2[e], lambda_k2[e], attn_subln[e], lam_init)
            rec = _lru(to_seq(xr), to_seq(yg), conv_w[e], conv_b[e], lru_w_a[e], lru_b_a[e],
                       lru_w_i[e], lru_b_i[e], lru_lambda[e])
            a, a_col = attn.reshape(T, width), 0
            b, b_col = rec.reshape(T, width), 0
            w_mix = w_mix_out[e]
        else:
            hn = _prenorm(xf, ln_mix_pre[l])
            f = _fourier(hn, B, S).reshape(T, D)
            a, a_col, b, b_col = f, 0, f, 1
            w_mix = w_fourier_out[l // 2]
        xf = _post_ffn(xf, a, a_col, b, b_col, w_mix.astype(BF16), ln_mix_post[l], ln_ffn_pre[l],
                       w_ffn_gate[l].astype(BF16), w_ffn_up[l].astype(BF16), w_ffn_down[l].astype(BF16),
                       ln_ffn_post[l])
    return xf.reshape(B, S, D)
```

```python
import functools

import numpy as np
import jax
import jax.numpy as jnp
from jax import lax
from jax.experimental import pallas as pl
from jax.experimental.pallas import tpu as pltpu

F32 = jnp.float32
BF16 = jnp.bfloat16

EPS = 1e-6
ATTN_HEADS = 4
QK_DIM = 64
HEAD_W = 2 * QK_DIM
LRU_HEADS = 4
LRU_HEAD_DIM = 128
CONV_WIDTH = 4
LRU_C = 8.0
FOURIER_GROUPS = 4
FOURIER_RADIX = 4

VMEM_LIMIT_BYTES = 54 * 1024 * 1024
POS_SPLIT = 64


def _rms(x, g):
    ms = jnp.mean(x * x, axis=-1, keepdims=True)
    return x * lax.rsqrt(ms + EPS) * g


def _sigmoid(x):
    return 1.0 / (1.0 + jnp.exp(-x))


def _params(*semantics):
    return pltpu.CompilerParams(dimension_semantics=semantics, vmem_limit_bytes=VMEM_LIMIT_BYTES)


def _resident(shape):
    nd = len(shape)
    return pl.BlockSpec(shape, lambda *_: (0,) * nd, pipeline_mode=pl.Buffered(1))


def _inproj_kernel(x_ref, g_ref, w_ref, q_ref, k_ref, v_ref, xr_ref, yg_ref):
    h = _rms(x_ref[...], g_ref[...]).astype(BF16)
    width = q_ref.shape[1]
    for n, o_ref in enumerate((q_ref, k_ref, v_ref, xr_ref, yg_ref)):
        z = jnp.dot(h, w_ref[:, n * width:(n + 1) * width], preferred_element_type=F32)
        if n == 0:
            z = z * (QK_DIM ** -0.5)
        o_ref[...] = z.astype(o_ref.dtype)


def _inproj(x, g, w, tm=512):
    T, D = x.shape
    width = w.shape[1] // 5
    row = lambda i: (i, 0)
    out_shapes = [jax.ShapeDtypeStruct((T, width), dt) for dt in (BF16, BF16, BF16, F32, F32)]
    return pl.pallas_call(
        _inproj_kernel,
        grid=(T // tm,),
        in_specs=[pl.BlockSpec((tm, D), row), _resident((1, D)), _resident(w.shape)],
        out_specs=[pl.BlockSpec((tm, width), row) for _ in out_shapes],
        out_shape=out_shapes,
        compiler_params=_params("parallel"),
        name="inproj",
    )(x, g.reshape(1, D), w)


ATTN_KEY_CHUNK = 256
BIAS_ROWS = 16
V_ONES_ROWS = 16


def _attn_kernel(lam_init, tq, q_ref, k_ref, v_ref, slope_ref, lq1_ref, lk1_ref, lq2_ref, lk2_ref,
                 g_ref, o_ref, ka_ref, vt_ref, corr_ref, s_ref, p_ref):
    S = k_ref.shape[1]
    tk = ATTN_KEY_CHUNK
    nchunk = S // tk
    ndiag = tq // tk
    i = pl.program_id(2)
    sig_row = slope_ref[0]
    sig = sig_row[:, 0:1]

    @pl.when(i == 0)
    def _build_keys_values():
        for c in range(nchunk):
            rows = slice(c * tk, (c + 1) * tk)
            kb = k_ref[0, rows, :].astype(F32)
            lane = lax.broadcasted_iota(jnp.int32, (tk, HEAD_W), 1)
            j = lax.broadcasted_iota(jnp.int32, (tk, HEAD_W), 0) + c * tk
            j_lo = j & (POS_SPLIT - 1)
            j_hi = j - j_lo
            aug = jnp.where(lane < 2, 1.0,
                            jnp.where(lane == 2, sig_row * j_lo.astype(F32),
                                      jnp.where(lane == 3, sig_row * j_hi.astype(F32), 0.0)))
            ka_ref[0, rows, 0:HEAD_W] = jnp.where(lane < QK_DIM, kb, 0.0).astype(BF16)
            ka_ref[1, rows, 0:HEAD_W] = jnp.where(lane >= QK_DIM, kb, 0.0).astype(BF16)
            ka_ref[0, rows, HEAD_W:2 * HEAD_W] = aug.astype(BF16)
            ka_ref[1, rows, HEAD_W:2 * HEAD_W] = aug.astype(BF16)
            sub = lax.broadcasted_iota(jnp.int32, (V_ONES_ROWS, tk), 0)
            vt_ref[c, 0:HEAD_W, :] = v_ref[0, rows, :].astype(F32).T.astype(BF16)
            vt_ref[c, HEAD_W:HEAD_W + V_ONES_ROWS, :] = jnp.where(sub == 0, 1.0, 0.0).astype(BF16)
        key_idx = lax.broadcasted_iota(jnp.int32, (tk, tq), 0)
        qry_idx = lax.broadcasted_iota(jnp.int32, (tk, tq), 1)
        for d in range(ndiag):
            rel = key_idx - qry_idx + d * tk
            corr_ref[d] = (-2.0 * sig) * jnp.maximum(rel, 0).astype(F32)

    i0 = i * tq
    q_t = q_ref[0].astype(F32).T.astype(BF16)
    sub = lax.broadcasted_iota(jnp.int32, (BIAS_ROWS, tq), 0)
    pos = lax.broadcasted_iota(jnp.int32, (BIAS_ROWS, tq), 1) + i0
    p_lo = pos & (POS_SPLIT - 1)
    p_hi = pos - p_lo
    bias_t = jnp.where(sub == 0, -sig * p_lo.astype(F32),
                       jnp.where(sub == 1, -sig * p_hi.astype(F32),
                                 jnp.where(sub < 4, 1.0, 0.0)))
    bias_left = bias_t.astype(BF16)
    bias_right = (-bias_t).astype(BF16)
    zero_rows = jnp.zeros((HEAD_W - BIAS_ROWS, tq), BF16)

    lam = (jnp.exp(jnp.sum(lq1_ref[...] * lk1_ref[...], axis=1, keepdims=True))
           - jnp.exp(jnp.sum(lq2_ref[...] * lk2_ref[...], axis=1, keepdims=True)) + lam_init)

    chunk_ids, starts, weights = [], [], []
    for r in range(nchunk):
        c = i * ndiag + r
        wrapped = c >= nchunk
        c = jnp.where(wrapped, c - nchunk, c)
        chunk_ids.append(c)
        starts.append(pl.multiple_of(c * tk, tk))
        bias = bias_left if r < ndiag else jnp.where(wrapped, bias_left, bias_right)
        weights.append(jnp.concatenate([q_t, bias, zero_rows], axis=0))

    def scores(m, r):
        s_t = jnp.dot(ka_ref[m, pl.ds(starts[r], tk), :], weights[r], preferred_element_type=F32)
        if r < ndiag:
            s_t = s_t + corr_ref[r]
        s_ref[m, r] = s_t
        return jnp.max(s_t.reshape(tk // 8, 8, tq), axis=0)

    def probs(m, r, mrow):
        p_ref[m, r] = jnp.exp(s_ref[m, r] - mrow).astype(BF16)

    def weighted_values(m, r, acc):
        part = jnp.dot(vt_ref[chunk_ids[r]], p_ref[m, r], preferred_element_type=F32)
        return part if acc is None else acc + part

    def running_max(mx, t):
        return t if mx is None else jnp.maximum(mx, t)

    mx0 = None
    for r in range(nchunk):
        mx0 = running_max(mx0, scores(0, r))
    mrow0 = jnp.max(mx0, axis=0, keepdims=True)
    mx1 = None
    for r in range(nchunk):
        mx1 = running_max(mx1, scores(1, r))
        probs(0, r, mrow0)
    mrow1 = jnp.max(mx1, axis=0, keepdims=True)
    acc0 = None
    for r in range(nchunk):
        acc0 = weighted_values(0, r, acc0)
        probs(1, r, mrow1)
    acc1 = None
    for r in range(nchunk):
        acc1 = weighted_values(1, r, acc1)
    out0 = acc0[0:HEAD_W] / acc0[HEAD_W:HEAD_W + 1]
    out1 = acc1[0:HEAD_W] / acc1[HEAD_W:HEAD_W + 1]
    o = (out0 - lam * out1).T
    o = _rms(o, g_ref[...]) * (1.0 - lam_init)
    o_ref[0] = o.astype(o_ref.dtype)


def _attention(q, k, v, slopes, lq1, lk1, lq2, lk2, subln_g, lam_init, tq=512):
    B, S, W = q.shape
    H = W // HEAD_W
    tk = ATTN_KEY_CHUNK
    kernel = functools.partial(_attn_kernel, lam_init, tq)
    vec = lambda a: a.reshape(1, -1).astype(F32)
    small = lambda n: pl.BlockSpec((1, n), lambda b, h, i: (0, 0))
    return pl.pallas_call(
        kernel,
        grid=(B, H, S // tq),
        in_specs=[
            pl.BlockSpec((1, tq, HEAD_W), lambda b, h, i: (b, i, h)),
            pl.BlockSpec((1, S, HEAD_W), lambda b, h, i: (b, 0, h)),
            pl.BlockSpec((1, S, HEAD_W), lambda b, h, i: (b, 0, h)),
            pl.BlockSpec((1, 1, HEAD_W), lambda b, h, i: (h, 0, 0)),
            small(QK_DIM), small(QK_DIM), small(QK_DIM), small(QK_DIM), small(HEAD_W),
        ],
        out_specs=pl.BlockSpec((1, tq, HEAD_W), lambda b, h, i: (b, i, h)),
        out_shape=jax.ShapeDtypeStruct((B, S, W), BF16),
        scratch_shapes=[
            pltpu.VMEM((2, S, 2 * HEAD_W), BF16),
            pltpu.VMEM((S // tk, HEAD_W + V_ONES_ROWS, tk), BF16),
            pltpu.VMEM((tq // tk, tk, tq), F32),
            pltpu.VMEM((2, S // tk, tk, tq), F32),
            pltpu.VMEM((2, S // tk, tk, tq), BF16),
        ],
        compiler_params=_params("parallel", "parallel", "arbitrary"),
        name="diff_attention",
    )(q, k, v, slopes, vec(lq1), vec(lk1), vec(lq2), vec(lk2), vec(subln_g))


def _scan8(a, b, reverse):
    row = lax.broadcasted_iota(jnp.int32, a.shape, 0)
    for k in (1, 2, 4):
        if reverse:
            keep = row < 8 - k
            shift = 8 - k
        else:
            keep = row >= k
            shift = k
        a_s = jnp.where(keep, pltpu.roll(a, shift, 0), 1.0)
        b_s = jnp.where(keep, pltpu.roll(b, shift, 0), 0.0)
        b = a * b_s + b
        a = a * a_s
    return a, b


def _lru_kernel(xr_ref, yg_ref, cw_ref, cb_ref, wg_ref, bg_ref, lam_ref, o_ref,
                xp_ref, af_ref, uf_ref, ab_ref, ub_ref):
    S = xr_ref.shape[1]
    C = LRU_HEAD_DIM
    tc = 512
    pad = 8
    xp_ref[0:pad, :] = jnp.zeros((pad, C), F32)
    xp_ref[S + pad:S + 2 * pad, :] = jnp.zeros((pad, C), F32)
    xp_ref[pad:S + pad, :] = xr_ref[0]

    lam = lam_ref[0]
    log_sig = jnp.minimum(lam, 0.0) - jnp.log1p(jnp.exp(-jnp.abs(lam)))
    cw = cw_ref[...]
    for c in range(S // tc):
        base = c * tc
        xc = cb_ref[...]
        for t in range(CONV_WIDTH):
            off = base + pad - CONV_WIDTH // 2 + t
            xc = xc + cw[t:t + 1, :] * xp_ref[off:off + tc, :]
        gates = jnp.dot(xc.astype(BF16), wg_ref[0], preferred_element_type=F32) + bg_ref[0]
        for d, (a_ref, u_ref) in enumerate(((af_ref, uf_ref), (ab_ref, ub_ref))):
            r = _sigmoid(gates[:, (2 * d) * C:(2 * d + 1) * C])
            ig = _sigmoid(gates[:, (2 * d + 1) * C:(2 * d + 2) * C])
            log_a = LRU_C * r * log_sig[:, d * C:(d + 1) * C]
            a = jnp.exp(log_a)
            one_minus_a2 = -jnp.tanh(log_a) * (1.0 + a * a)
            a_ref[base:base + tc, :] = a
            u_ref[base:base + tc, :] = jnp.sqrt(one_minus_a2) * (ig * xc)

    unroll = 8
    span = 8 * unroll

    def body(it, carry):
        cf, cb = carry
        f0 = pl.multiple_of(it * span, span)
        b0 = pl.multiple_of(S - (it + 1) * span, span)
        af = af_ref[pl.ds(f0, span), :]
        uf = uf_ref[pl.ds(f0, span), :]
        hs = []
        for j in range(unroll):
            a, b = _scan8(af[8 * j:8 * j + 8], uf[8 * j:8 * j + 8], False)
            hs.append(a * cf + b)
            cf = jnp.broadcast_to(a[7:8], (8, C)) * cf + jnp.broadcast_to(b[7:8], (8, C))
        uf_ref[pl.ds(f0, span), :] = jnp.concatenate(hs, axis=0)
        ab = ab_ref[pl.ds(b0, span), :]
        ub = ub_ref[pl.ds(b0, span), :]
        hs = []
        for j in reversed(range(unroll)):
            a, b = _scan8(ab[8 * j:8 * j + 8], ub[8 * j:8 * j + 8], True)
            hs.append(a * cb + b)
            cb = jnp.broadcast_to(a[0:1], (8, C)) * cb + jnp.broadcast_to(b[0:1], (8, C))
        ub_ref[pl.ds(b0, span), :] = jnp.concatenate(hs[::-1], axis=0)
        return cf, cb

    zero = jnp.zeros((8, C), F32)
    lax.fori_loop(0, S // span, body, (zero, zero))

    for c in range(S // tc):
        rows = slice(c * tc, (c + 1) * tc)
        y = yg_ref[0, rows, :]
        gelu = 0.5 * y * (1.0 + jnp.tanh(0.7978845608028654 * (y + 0.044715 * (y * y * y))))
        o_ref[0, rows, :] = ((uf_ref[rows, :] + ub_ref[rows, :]) * gelu).astype(o_ref.dtype)


def _lru(xr, yg, conv_w, conv_b, w_a, b_a, w_i, b_i, lam):
    B, S, W = xr.shape
    C = LRU_HEAD_DIM
    H = W // C
    wg = jnp.concatenate([w_a[0], w_i[0], w_a[1], w_i[1]], axis=-1).astype(BF16)
    per_head = lambda a: a.reshape(H, 1, C)
    bg = jnp.concatenate([per_head(b_a[0]), per_head(b_i[0]), per_head(b_a[1]), per_head(b_i[1])], axis=-1)
    lam2 = jnp.concatenate([per_head(lam[0]), per_head(lam[1])], axis=-1)
    seq = pl.BlockSpec((1, S, C), lambda b, h: (b, 0, h))
    return pl.pallas_call(
        _lru_kernel,
        grid=(B, H),
        in_specs=[
            seq, seq,
            pl.BlockSpec((CONV_WIDTH, C), lambda b, h: (0, h)),
            pl.BlockSpec((1, C), lambda b, h: (0, h)),
            pl.BlockSpec((1, C, 4 * C), lambda b, h: (h, 0, 0)),
            pl.BlockSpec((1, 1, 4 * C), lambda b, h: (h, 0, 0)),
            pl.BlockSpec((1, 1, 2 * C), lambda b, h: (h, 0, 0)),
        ],
        out_specs=seq,
        out_shape=jax.ShapeDtypeStruct((B, S, W), BF16),
        scratch_shapes=[pltpu.VMEM((S + 16, C), F32)] + [pltpu.VMEM((S, C), F32) for _ in range(4)],
        compiler_params=_params("parallel", "parallel"),
        name="rg_lru",
    )(xr, yg, conv_w, conv_b.reshape(1, W), wg, bg, lam2)


def _prenorm_kernel(x_ref, g_ref, o_ref):
    o_ref[...] = _rms(x_ref[...], g_ref[...]).astype(o_ref.dtype)


def _prenorm(x, g, tm=1024):
    T, D = x.shape
    row = lambda i: (i, 0)
    return pl.pallas_call(
        _prenorm_kernel,
        grid=(T // tm,),
        in_specs=[pl.BlockSpec((tm, D), row), _resident((1, D))],
        out_specs=pl.BlockSpec((tm, D), row),
        out_shape=jax.ShapeDtypeStruct((T, D), BF16),
        compiler_params=_params("parallel"),
        name="prenorm",
    )(x, g.reshape(1, D))


@functools.lru_cache(maxsize=None)
def _dft_tables(seq, group_dim):
    n4 = seq // FOURIER_RADIX
    k1 = np.arange(n4, dtype=np.int64)[:, None]
    s1 = np.arange(n4, dtype=np.int64)[None, :]
    cos_t, sin_t = [], []
    for s2 in range(FOURIER_RADIX):
        ang = 2.0 * np.pi * ((k1 * (FOURIER_RADIX * s1 + s2)) % seq).astype(np.float64) / seq
        cos_t.append(np.cos(ang) / np.sqrt(seq))
        sin_t.append(-np.sin(ang) / np.sqrt(seq))
    c = np.arange(group_dim, dtype=np.int64)
    ang = 2.0 * np.pi * ((c[:, None] * c[None, :]) % group_dim).astype(np.float64) / group_dim
    chan = np.concatenate([np.cos(ang), np.sin(ang)], axis=0) / np.sqrt(group_dim)
    return (np.stack(cos_t).astype(np.float32), np.stack(sin_t).astype(np.float32), chan.astype(np.float32))


def _fourier_kernel(h0_ref, h1_ref, h2_ref, h3_ref, fc_ref, fs_ref, cs_ref, o_ref):
    n4 = fc_ref.shape[1]
    rows = 256
    for kc in range(n4 // rows):
        rs = slice(kc * rows, (kc + 1) * rows)
        yr, yi = [], []
        for r, h_ref in enumerate((h0_ref, h1_ref, h2_ref, h3_ref)):
            x = h_ref[0]
            yr.append(jnp.dot(fc_ref[r, rs, :], x, preferred_element_type=F32))
            yi.append(jnp.dot(fs_ref[r, rs, :], x, preferred_element_type=F32))
        xs = (
            (yr[0] + yr[1] + yr[2] + yr[3], yi[0] + yi[1] + yi[2] + yi[3]),
            (yr[0] + yi[1] - yr[2] - yi[3], yi[0] - yr[1] - yi[2] + yr[3]),
            (yr[0] - yr[1] + yr[2] - yr[3], yi[0] - yi[1] + yi[2] - yi[3]),
            (yr[0] - yi[1] - yr[2] + yi[3], yi[0] + yr[1] - yi[2] - yr[3]),
        )
        for k2, (xr, xi) in enumerate(xs):
            z = jnp.concatenate([xr.astype(BF16), xi.astype(BF16)], axis=1)
            f = jnp.dot(z, cs_ref[...], preferred_element_type=F32)
            o_ref[0, k2 * n4 + kc * rows:k2 * n4 + (kc + 1) * rows, :] = f.astype(o_ref.dtype)


def _fourier(hn, B, S):
    T, D = hn.shape
    G = D // FOURIER_GROUPS
    n4 = S // FOURIER_RADIX
    fc, fs, cs = (jnp.asarray(t).astype(BF16) for t in _dft_tables(S, G))
    h4 = hn.reshape(B, n4, FOURIER_RADIX * D)
    per_res = lambda r: pl.BlockSpec((1, n4, G), lambda b, g: (b, 0, r * FOURIER_GROUPS + g))
    return pl.pallas_call(
        _fourier_kernel,
        grid=(B, FOURIER_GROUPS),
        in_specs=[per_res(0), per_res(1), per_res(2), per_res(3),
                  _resident(fc.shape), _resident(fs.shape), _resident(cs.shape)],
        out_specs=pl.BlockSpec((1, S, G), lambda b, g: (b, 0, g)),
        out_shape=jax.ShapeDtypeStruct((B, S, D), BF16),
        compiler_params=_params("parallel", "parallel"),
        name="fourier",
    )(h4, h4, h4, h4, fc, fs, cs)


def _post_ffn_kernel(x_ref, a_ref, b_ref, wa_ref, wb_ref, gpost_ref, gpre_ref, wg_ref, wu_ref, wd_ref,
                     gfpost_ref, o_ref, act_ref):
    m = (jnp.dot(a_ref[...], wa_ref[...], preferred_element_type=F32)
         + jnp.dot(b_ref[...], wb_ref[...], preferred_element_type=F32))
    x = x_ref[...] + _rms(m, gpost_ref[...])
    h = _rms(x, gpre_ref[...]).astype(BF16)
    dff = wg_ref.shape[1]
    ch = 256
    for c in range(dff // ch):
        cols = slice(c * ch, (c + 1) * ch)
        g = jnp.dot(h, wg_ref[:, cols], preferred_element_type=F32)
        u = jnp.dot(h, wu_ref[:, cols], preferred_element_type=F32)
        act_ref[:, cols] = (g * _sigmoid(g) * u).astype(BF16)
    y = jnp.dot(act_ref[...], wd_ref[...], preferred_element_type=F32)
    o_ref[...] = x + _rms(y, gfpost_ref[...])


def _post_ffn(x, a, a_col, b, b_col, w_mix, g_post, g_pre, wg, wu, wd, g_fpost, tm=512):
    T, D = x.shape
    half = w_mix.shape[0] // 2
    dff = wg.shape[1]
    row = lambda i: (i, 0)
    vec = lambda g: g.reshape(1, D)
    return pl.pallas_call(
        _post_ffn_kernel,
        grid=(T // tm,),
        in_specs=[
            pl.BlockSpec((tm, D), row),
            pl.BlockSpec((tm, half), lambda i: (i, a_col)),
            pl.BlockSpec((tm, half), lambda i: (i, b_col)),
            pl.BlockSpec((half, D), lambda i: (0, 0), pipeline_mode=pl.Buffered(1)),
            pl.BlockSpec((half, D), lambda i: (1, 0), pipeline_mode=pl.Buffered(1)),
            _resident((1, D)), _resident((1, D)),
            _resident(wg.shape), _resident(wu.shape), _resident(wd.shape),
            _resident((1, D)),
        ],
        out_specs=pl.BlockSpec((tm, D), row),
        out_shape=jax.ShapeDtypeStruct((T, D), F32),
        scratch_shapes=[pltpu.VMEM((tm, dff), BF16)],
        compiler_params=_params("parallel"),
        name="post_ffn",
    )(x, a, b, w_mix, w_mix, vec(g_post), vec(g_pre), wg, wu, wd, vec(g_fpost))


def kernel(x, ln_mix_pre, ln_mix_post, ln_ffn_pre, ln_ffn_post, w_in, w_mix_out, lambda_q1, lambda_k1,
           lambda_q2, lambda_k2, attn_subln, conv_w, conv_b, lru_w_a, lru_b_a, lru_w_i, lru_b_i, lru_lambda,
           w_fourier_out, w_ffn_gate, w_ffn_up, w_ffn_down):
    B, S, D = x.shape
    T = B * S
    depth = ln_mix_pre.shape[0]
    heads = jnp.arange(1, ATTN_HEADS + 1, dtype=F32)
    slopes = jnp.broadcast_to(jnp.exp2(-8.0 * heads / ATTN_HEADS)[:, None, None], (ATTN_HEADS, 1, HEAD_W))
    xf = x.reshape(T, D)
    for l in range(depth):
        if l % 2 == 0:
            e = l // 2
            lam_init = 0.8 - 0.6 * float(np.exp(-0.3 * l))
            q, k, v, xr, yg = _inproj(xf, ln_mix_pre[l], w_in[e].astype(BF16))
            width = q.shape[1]
            to_seq = lambda a: a.reshape(B, S, width)
            attn = _attention(to_seq(q), to_seq(k), to_seq(v), slopes, lambda_q1[e], lambda_k1[e],
                              lambda_q2[e], lambda_k2[e], attn_subln[e], lam_init)
            rec = _lru(to_seq(xr), to_seq(yg), conv_w[e], conv_b[e], lru_w_a[e], lru_b_a[e],
                       lru_w_i[e], lru_b_i[e], lru_lambda[e])
            a, a_col = attn.reshape(T, width), 0
            b, b_col = rec.reshape(T, width), 0
            w_mix = w_mix_out[e]
        else:
            hn = _prenorm(xf, ln_mix_pre[l])
            f = _fourier(hn, B, S).reshape(T, D)
            a, a_col, b, b_col = f, 0, f, 1
            w_mix = w_fourier_out[l // 2]
        xf = _post_ffn(xf, a, a_col, b, b_col, w_mix.astype(BF16), ln_mix_post[l], ln_ffn_pre[l],
                       w_ffn_gate[l].astype(BF16), w_ffn_up[l].astype(BF16), w_ffn_down[l].astype(BF16),
                       ln_ffn_post[l])
    return xf.reshape(B, S, D)
```

```python
import functools

import numpy as np
import jax
import jax.numpy as jnp
from jax import lax
from jax.experimental import pallas as pl
from jax.experimental.pallas import tpu as pltpu

F32 = jnp.float32
BF16 = jnp.bfloat16

EPS = 1e-6
ATTN_HEADS = 4
QK_DIM = 64
HEAD_W = 2 * QK_DIM
LRU_HEADS = 4
LRU_HEAD_DIM = 128
CONV_WIDTH = 4
LRU_C = 8.0
FOURIER_GROUPS = 4
FOURIER_RADIX = 4

VMEM_LIMIT_BYTES = 54 * 1024 * 1024
POS_SPLIT = 64


def _rms(x, g):
    ms = jnp.mean(x * x, axis=-1, keepdims=True)
    return x * lax.rsqrt(ms + EPS) * g


def _sigmoid(x):
    return 1.0 / (1.0 + jnp.exp(-x))


def _params(*semantics, flags=None):
    return pltpu.CompilerParams(dimension_semantics=semantics, vmem_limit_bytes=VMEM_LIMIT_BYTES, flags=flags)


def _resident(shape):
    nd = len(shape)
    return pl.BlockSpec(shape, lambda *_: (0,) * nd, pipeline_mode=pl.Buffered(1))


def _inproj_kernel(x_ref, g_ref, w_ref, q_ref, k_ref, v_ref, xr_ref, yg_ref):
    h = _rms(x_ref[...], g_ref[...]).astype(BF16)
    width = q_ref.shape[1]
    for n, o_ref in enumerate((q_ref, k_ref, v_ref, xr_ref, yg_ref)):
        z = jnp.dot(h, w_ref[:, n * width:(n + 1) * width], preferred_element_type=F32)
        if n == 0:
            z = z * (QK_DIM ** -0.5)
        o_ref[...] = z.astype(o_ref.dtype)


def _inproj(x, g, w, tm=512):
    T, D = x.shape
    width = w.shape[1] // 5
    row = lambda i: (i, 0)
    out_shapes = [jax.ShapeDtypeStruct((T, width), dt) for dt in (BF16, BF16, BF16, F32, F32)]
    return pl.pallas_call(
        _inproj_kernel,
        grid=(T // tm,),
        in_specs=[pl.BlockSpec((tm, D), row), _resident((1, D)), _resident(w.shape)],
        out_specs=[pl.BlockSpec((tm, width), row) for _ in out_shapes],
        out_shape=out_shapes,
        compiler_params=_params("parallel"),
        name="inproj",
    )(x, g.reshape(1, D), w)


ATTN_KEY_CHUNK = 256
ATTN_KEY_SECTIONS = 2
BIAS_ROWS = 16
V_ONES_ROWS = 16


def _attn_kernel(lam_init, tq, q_ref, k_ref, v_ref, slope_ref, lq1_ref, lk1_ref, lq2_ref, lk2_ref,
                 g_ref, o_ref, ka_ref, vt_ref, corr_ref, s_ref):
    S = k_ref.shape[1]
    tk = ATTN_KEY_CHUNK
    nchunk = S // tk
    ndiag = tq // tk
    i = pl.program_id(2)
    sig_row = slope_ref[0]
    sig = sig_row[:, 0:1]

    @pl.when(i == 0)
    def _build_keys_values():
        for c in range(nchunk):
            rows = slice(c * tk, (c + 1) * tk)
            kb = k_ref[0, rows, :].astype(F32)
            lane = lax.broadcasted_iota(jnp.int32, (tk, HEAD_W), 1)
            j = lax.broadcasted_iota(jnp.int32, (tk, HEAD_W), 0) + c * tk
            j_lo = j & (POS_SPLIT - 1)
            j_hi = j - j_lo
            aug = jnp.where(lane < 2, 1.0,
                            jnp.where(lane == 2, sig_row * j_lo.astype(F32),
                                      jnp.where(lane == 3, sig_row * j_hi.astype(F32), 0.0)))
            ka_ref[0, rows, 0:HEAD_W] = jnp.where(lane < QK_DIM, kb, 0.0).astype(BF16)
            ka_ref[1, rows, 0:HEAD_W] = jnp.where(lane >= QK_DIM, kb, 0.0).astype(BF16)
            ka_ref[0, rows, HEAD_W:2 * HEAD_W] = aug.astype(BF16)
            ka_ref[1, rows, HEAD_W:2 * HEAD_W] = aug.astype(BF16)
            sub = lax.broadcasted_iota(jnp.int32, (V_ONES_ROWS, tk), 0)
            vt_ref[c, 0:HEAD_W, :] = v_ref[0, rows, :].astype(F32).T.astype(BF16)
            vt_ref[c, HEAD_W:HEAD_W + V_ONES_ROWS, :] = jnp.where(sub == 0, 1.0, 0.0).astype(BF16)
        key_idx = lax.broadcasted_iota(jnp.int32, (tk, tq), 0)
        qry_idx = lax.broadcasted_iota(jnp.int32, (tk, tq), 1)
        for d in range(ndiag):
            rel = key_idx - qry_idx + d * tk
            corr_ref[d] = (-2.0 * sig) * jnp.maximum(rel, 0).astype(F32)

    i0 = i * tq
    q_t = q_ref[0].astype(F32).T.astype(BF16)
    sub = lax.broadcasted_iota(jnp.int32, (BIAS_ROWS, tq), 0)
    pos = lax.broadcasted_iota(jnp.int32, (BIAS_ROWS, tq), 1) + i0
    p_lo = pos & (POS_SPLIT - 1)
    p_hi = pos - p_lo
    bias_t = jnp.where(sub == 0, -sig * p_lo.astype(F32),
                       jnp.where(sub == 1, -sig * p_hi.astype(F32),
                                 jnp.where(sub < 4, 1.0, 0.0)))
    bias_left = bias_t.astype(BF16)
    bias_right = (-bias_t).astype(BF16)
    zero_rows = jnp.zeros((HEAD_W - BIAS_ROWS, tq), BF16)

    lam = (jnp.exp(jnp.sum(lq1_ref[...] * lk1_ref[...], axis=1, keepdims=True))
           - jnp.exp(jnp.sum(lq2_ref[...] * lk2_ref[...], axis=1, keepdims=True)) + lam_init)

    chunk_ids, starts, weights = [], [], []
    for r in range(nchunk):
        c = i * ndiag + r
        wrapped = c >= nchunk
        c = jnp.where(wrapped, c - nchunk, c)
        chunk_ids.append(c)
        starts.append(pl.multiple_of(c * tk, tk))
        bias = bias_left if r < ndiag else jnp.where(wrapped, bias_left, bias_right)
        weights.append(jnp.concatenate([q_t, bias, zero_rows], axis=0))

    def scores(m, r):
        s_t = jnp.dot(ka_ref[m, pl.ds(starts[r], tk), :], weights[r], preferred_element_type=F32)
        if r < ndiag:
            s_t = s_t + corr_ref[r]
        s_ref[m, r] = s_t
        return jnp.max(s_t.reshape(tk // 8, 8, tq), axis=0)

    def probs(m, r, mrow):
        return jnp.exp((s_ref[m, r] - mrow).astype(BF16))

    def weighted_values(r, p, acc):
        part = jnp.dot(vt_ref[chunk_ids[r]], p, preferred_element_type=F32)
        return part if acc is None else acc + part

    def running_max(mx, t):
        return t if mx is None else jnp.maximum(mx, t)

    def unfoldable_zero(x):
        return pltpu.bitcast(lax.shift_right_logical(pltpu.bitcast(x, jnp.uint32), jnp.uint32(32)), F32)

    per_sec = nchunk // ATTN_KEY_SECTIONS
    jobs = [(sec, m) for sec in range(ATTN_KEY_SECTIONS) for m in range(2)]
    row_max = [None, None]
    acc = [None, None]

    def start_softmax(job, mx):
        _, m = job
        new = jnp.max(mx, axis=0, keepdims=True)
        if row_max[m] is not None:
            new = jnp.maximum(new, row_max[m])
            acc[m] = acc[m] * jnp.exp(row_max[m] - new)
        row_max[m] = new

    lag = 2
    prev = None
    for job in jobs + [None]:
        mx, pending, pace = None, [], None
        if prev is not None:
            pace = row_max[prev[1]]
        for c in range(per_sec):
            if job is not None:
                tile_mx = scores(job[1], job[0] * per_sec + c)
                mx = running_max(mx, tile_mx)
            if prev is not None:
                r = prev[0] * per_sec + c
                pending.append(probs(prev[1], r, pace))
                if job is not None:
                    pace = row_max[prev[1]] + unfoldable_zero(tile_mx[0:1])
                if c >= lag:
                    acc[prev[1]] = weighted_values(r - lag, pending[c - lag], acc[prev[1]])
        if prev is not None:
            for c in range(per_sec - lag, per_sec):
                acc[prev[1]] = weighted_values(prev[0] * per_sec + c, pending[c], acc[prev[1]])
        if job is not None:
            start_softmax(job, mx)
        prev = job
    out0 = acc[0][0:HEAD_W] / acc[0][HEAD_W:HEAD_W + 1]
    out1 = acc[1][0:HEAD_W] / acc[1][HEAD_W:HEAD_W + 1]
    o = (out0 - lam * out1).T
    o = _rms(o, g_ref[...]) * (1.0 - lam_init)
    o_ref[0] = o.astype(o_ref.dtype)


def _attention(q, k, v, slopes, lq1, lk1, lq2, lk2, subln_g, lam_init, tq=512):
    B, S, W = q.shape
    H = W // HEAD_W
    tk = ATTN_KEY_CHUNK
    kernel = functools.partial(_attn_kernel, lam_init, tq)
    vec = lambda a: a.reshape(1, -1).astype(F32)
    small = lambda n: pl.BlockSpec((1, n), lambda b, h, i: (0, 0))
    return pl.pallas_call(
        kernel,
        grid=(B, H, S // tq),
        in_specs=[
            pl.BlockSpec((1, tq, HEAD_W), lambda b, h, i: (b, i, h)),
            pl.BlockSpec((1, S, HEAD_W), lambda b, h, i: (b, 0, h)),
            pl.BlockSpec((1, S, HEAD_W), lambda b, h, i: (b, 0, h)),
            pl.BlockSpec((1, 1, HEAD_W), lambda b, h, i: (h, 0, 0)),
            small(QK_DIM), small(QK_DIM), small(QK_DIM), small(QK_DIM), small(HEAD_W),
        ],
        out_specs=pl.BlockSpec((1, tq, HEAD_W), lambda b, h, i: (b, i, h)),
        out_shape=jax.ShapeDtypeStruct((B, S, W), BF16),
        scratch_shapes=[
            pltpu.VMEM((2, S, 2 * HEAD_W), BF16),
            pltpu.VMEM((S // tk, HEAD_W + V_ONES_ROWS, tk), BF16),
            pltpu.VMEM((tq // tk, tk, tq), F32),
            pltpu.VMEM((2, S // tk, tk, tq), F32),
        ],
        compiler_params=_params("parallel", "parallel", "arbitrary"),
        name="diff_attention",
    )(q, k, v, slopes, vec(lq1), vec(lk1), vec(lq2), vec(lk2), vec(subln_g))


def _scan8(a, b, reverse):
    row = lax.broadcasted_iota(jnp.int32, a.shape, 0)
    for k in (1, 2, 4):
        if reverse:
            keep = row < 8 - k
            shift = 8 - k
        else:
            keep = row >= k
            shift = k
        a_s = jnp.where(keep, pltpu.roll(a, shift, 0), 1.0)
        b_s = jnp.where(keep, pltpu.roll(b, shift, 0), 0.0)
        b = a * b_s + b
        a = a * a_s
    return a, b


def _lru_kernel(xr_ref, yg_ref, cw_ref, cb_ref, wg_ref, bg_ref, lam_ref, o_ref,
                xp_ref, af_ref, uf_ref, ab_ref, ub_ref):
    S = xr_ref.shape[1]
    C = LRU_HEAD_DIM
    tc = 512
    pad = 8
    xp_ref[0:pad, :] = jnp.zeros((pad, C), F32)
    xp_ref[S + pad:S + 2 * pad, :] = jnp.zeros((pad, C), F32)
    xp_ref[pad:S + pad, :] = xr_ref[0]

    lam = lam_ref[0]
    log_sig = jnp.minimum(lam, 0.0) - jnp.log1p(jnp.exp(-jnp.abs(lam)))
    cw = cw_ref[...]
    for c in range(S // tc):
        base = c * tc
        xc = cb_ref[...]
        for t in range(CONV_WIDTH):
            off = base + pad - CONV_WIDTH // 2 + t
            xc = xc + cw[t:t + 1, :] * xp_ref[off:off + tc, :]
        gates = jnp.dot(xc.astype(BF16), wg_ref[0], preferred_element_type=F32) + bg_ref[0]
        for d, (a_ref, u_ref) in enumerate(((af_ref, uf_ref), (ab_ref, ub_ref))):
            r = _sigmoid(gates[:, (2 * d) * C:(2 * d + 1) * C])
            ig = _sigmoid(gates[:, (2 * d + 1) * C:(2 * d + 2) * C])
            log_a = LRU_C * r * log_sig[:, d * C:(d + 1) * C]
            a = jnp.exp(log_a)
            one_minus_a2 = -jnp.tanh(log_a) * (1.0 + a * a)
            a_ref[base:base + tc, :] = a
            u_ref[base:base + tc, :] = jnp.sqrt(one_minus_a2) * (ig * xc)

    unroll = 8
    span = 8 * unroll

    def body(it, carry):
        cf, cb = carry
        f0 = pl.multiple_of(it * span, span)
        b0 = pl.multiple_of(S - (it + 1) * span, span)
        af = af_ref[pl.ds(f0, span), :]
        uf = uf_ref[pl.ds(f0, span), :]
        hs = []
        for j in range(unroll):
            a, b = _scan8(af[8 * j:8 * j + 8], uf[8 * j:8 * j + 8], False)
            hs.append(a * cf + b)
            cf = jnp.broadcast_to(a[7:8], (8, C)) * cf + jnp.broadcast_to(b[7:8], (8, C))
        uf_ref[pl.ds(f0, span), :] = jnp.concatenate(hs, axis=0)
        ab = ab_ref[pl.ds(b0, span), :]
        ub = ub_ref[pl.ds(b0, span), :]
        hs = []
        for j in reversed(range(unroll)):
            a, b = _scan8(ab[8 * j:8 * j + 8], ub[8 * j:8 * j + 8], True)
            hs.append(a * cb + b)
            cb = jnp.broadcast_to(a[0:1], (8, C)) * cb + jnp.broadcast_to(b[0:1], (8, C))
        ub_ref[pl.ds(b0, span), :] = jnp.concatenate(hs[::-1], axis=0)
        return cf, cb

    zero = jnp.zeros((8, C), F32)
    lax.fori_loop(0, S // span, body, (zero, zero))

    for c in range(S // tc):
        rows = slice(c * tc, (c + 1) * tc)
        y = yg_ref[0, rows, :]
        gelu = 0.5 * y * (1.0 + jnp.tanh(0.7978845608028654 * (y + 0.044715 * (y * y * y))))
        o_ref[0, rows, :] = ((uf_ref[rows, :] + ub_ref[rows, :]) * gelu).astype(o_ref.dtype)


def _lru(xr, yg, conv_w, conv_b, w_a, b_a, w_i, b_i, lam):
    B, S, W = xr.shape
    C = LRU_HEAD_DIM
    H = W // C
    wg = jnp.concatenate([w_a[0], w_i[0], w_a[1], w_i[1]], axis=-1).astype(BF16)
    per_head = lambda a: a.reshape(H, 1, C)
    bg = jnp.concatenate([per_head(b_a[0]), per_head(b_i[0]), per_head(b_a[1]), per_head(b_i[1])], axis=-1)
    lam2 = jnp.concatenate([per_head(lam[0]), per_head(lam[1])], axis=-1)
    seq = pl.BlockSpec((1, S, C), lambda b, h: (b, 0, h))
    return pl.pallas_call(
        _lru_kernel,
        grid=(B, H),
        in_specs=[
            seq, seq,
            pl.BlockSpec((CONV_WIDTH, C), lambda b, h: (0, h)),
            pl.BlockSpec((1, C), lambda b, h: (0, h)),
            pl.BlockSpec((1, C, 4 * C), lambda b, h: (h, 0, 0)),
            pl.BlockSpec((1, 1, 4 * C), lambda b, h: (h, 0, 0)),
            pl.BlockSpec((1, 1, 2 * C), lambda b, h: (h, 0, 0)),
        ],
        out_specs=seq,
        out_shape=jax.ShapeDtypeStruct((B, S, W), BF16),
        scratch_shapes=[pltpu.VMEM((S + 16, C), F32)] + [pltpu.VMEM((S, C), F32) for _ in range(4)],
        compiler_params=_params("parallel", "parallel"),
        name="rg_lru",
    )(xr, yg, conv_w, conv_b.reshape(1, W), wg, bg, lam2)


@functools.lru_cache(maxsize=None)
def _dft_tables(seq, group_dim):
    n4 = seq // FOURIER_RADIX
    k1 = np.arange(n4, dtype=np.int64)[:, None]
    s1 = np.arange(n4, dtype=np.int64)[None, :]
    cos_t, sin_t = [], []
    for s2 in range(FOURIER_RADIX):
        ang = 2.0 * np.pi * ((k1 * (FOURIER_RADIX * s1 + s2)) % seq).astype(np.float64) / seq
        cos_t.append(np.cos(ang) / np.sqrt(seq))
        sin_t.append(-np.sin(ang) / np.sqrt(seq))
    c = np.arange(group_dim, dtype=np.int64)
    ang = 2.0 * np.pi * ((c[:, None] * c[None, :]) % group_dim).astype(np.float64) / group_dim
    chan = np.concatenate([np.cos(ang), np.sin(ang)], axis=0) / np.sqrt(group_dim)
    return (np.stack(cos_t).astype(np.float32), np.stack(sin_t).astype(np.float32), chan.astype(np.float32))


def _fourier_kernel(h0_ref, h1_ref, h2_ref, h3_ref, fc_ref, fs_ref, cs_ref, o_ref):
    n4 = fc_ref.shape[1]
    rows = 256
    for kc in range(n4 // rows):
        rs = slice(kc * rows, (kc + 1) * rows)
        yr, yi = [], []
        for r, h_ref in enumerate((h0_ref, h1_ref, h2_ref, h3_ref)):
            x = h_ref[0, 0]
            yr.append(jnp.dot(fc_ref[r, rs, :], x, preferred_element_type=F32))
            yi.append(jnp.dot(fs_ref[r, rs, :], x, preferred_element_type=F32))
        xs = (
            (yr[0] + yr[1] + yr[2] + yr[3], yi[0] + yi[1] + yi[2] + yi[3]),
            (yr[0] + yi[1] - yr[2] - yi[3], yi[0] - yr[1] - yi[2] + yr[3]),
            (yr[0] - yr[1] + yr[2] - yr[3], yi[0] - yi[1] + yi[2] - yi[3]),
            (yr[0] - yi[1] - yr[2] + yi[3], yi[0] + yr[1] - yi[2] - yr[3]),
        )
        for k2, (xr, xi) in enumerate(xs):
            z = jnp.concatenate([xr.astype(BF16), xi.astype(BF16)], axis=1)
            f = jnp.dot(z, cs_ref[...], preferred_element_type=F32)
            o_ref[0, k2 * n4 + kc * rows:k2 * n4 + (kc + 1) * rows, :] = f.astype(o_ref.dtype)


def _fourier(h4):
    B, _, n4, D = h4.shape
    S = n4 * FOURIER_RADIX
    G = D // FOURIER_GROUPS
    fc, fs, cs = (jnp.asarray(t).astype(BF16) for t in _dft_tables(S, G))
    per_res = lambda r: pl.BlockSpec((1, 1, n4, G), lambda b, g: (b, r, 0, g))
    return pl.pallas_call(
        _fourier_kernel,
        grid=(B, FOURIER_GROUPS),
        in_specs=[per_res(0), per_res(1), per_res(2), per_res(3),
                  _resident(fc.shape), _resident(fs.shape), _resident(cs.shape)],
        out_specs=pl.BlockSpec((1, S, G), lambda b, g: (b, 0, g)),
        out_shape=jax.ShapeDtypeStruct((B, S, D), BF16),
        compiler_params=_params("parallel", "parallel"),
        name="fourier",
    )(h4, h4, h4, h4, fc, fs, cs)


FFN_ROW_SPLITS = 2
FFN_COL_CHUNK = 256


def _post_ffn_kernel(emit_next, x_ref, a_ref, b_ref, wa_ref, wb_ref, gpost_ref, gpre_ref, wg_ref, wu_ref,
                     wd_ref, gfpost_ref, gnext_ref, pick_ref, o_ref, *rest):
    act_ref = rest[-1]
    tm = x_ref.shape[0]
    dff = wg_ref.shape[1]
    groups = [slice(s * (tm // FFN_ROW_SPLITS), (s + 1) * (tm // FFN_ROW_SPLITS)) for s in range(FFN_ROW_SPLITS)]

    def mix(rows):
        return (jnp.dot(a_ref[rows, :], wa_ref[...], preferred_element_type=F32)
                + jnp.dot(b_ref[rows, :], wb_ref[...], preferred_element_type=F32))

    def gate_up(rows, h):
        for c in range(dff // FFN_COL_CHUNK):
            cols = slice(c * FFN_COL_CHUNK, (c + 1) * FFN_COL_CHUNK)
            g = jnp.dot(h, wg_ref[:, cols], preferred_element_type=F32)
            u = jnp.dot(h, wu_ref[:, cols], preferred_element_type=F32)
            act_ref[rows, cols] = (g * _sigmoid(g) * u).astype(BF16)

    mixed = [mix(rows) for rows in groups]
    xs = []
    for rows, m in zip(groups, mixed):
        x = x_ref[rows, :] + _rms(m, gpost_ref[...])
        xs.append(x)
        gate_up(rows, _rms(x, gpre_ref[...]).astype(BF16))
    for s, (rows, x) in enumerate(zip(groups, xs)):
        y = jnp.dot(act_ref[rows, :], wd_ref[...], preferred_element_type=F32)
        x = x + _rms(y, gfpost_ref[...])
        o_ref[rows, :] = x
        if emit_next:
            hn_ref = rest[0]
            hn = _rms(x, gnext_ref[...]).astype(BF16)
            n = hn.shape[0] // FOURIER_RADIX
            for r in range(FOURIER_RADIX):
                picked = jnp.dot(pick_ref[r], hn, preferred_element_type=F32)
                hn_ref[0, r, s * n:(s + 1) * n, :] = picked.astype(hn_ref.dtype)


def _post_ffn(x, a, a_col, b, b_col, w_mix, g_post, g_pre, wg, wu, wd, g_fpost, g_next, seq, tm=512):
    T, D = x.shape
    half = w_mix.shape[0] // 2
    dff = wg.shape[1]
    emit_next = g_next is not None
    rows_per_group = tm // FFN_ROW_SPLITS
    n_pick = rows_per_group // FOURIER_RADIX
    pick = np.zeros((FOURIER_RADIX, n_pick, rows_per_group), np.float32)
    for r in range(FOURIER_RADIX):
        pick[r, np.arange(n_pick), FOURIER_RADIX * np.arange(n_pick) + r] = 1.0
    pick = jnp.asarray(pick, BF16)
    row = lambda i: (i, 0)
    vec = lambda g: g.reshape(1, D)
    out_specs = [pl.BlockSpec((tm, D), row)]
    out_shape = [jax.ShapeDtypeStruct((T, D), F32)]
    if emit_next:
        steps = seq // tm
        out_specs.append(pl.BlockSpec((1, FOURIER_RADIX, tm // FOURIER_RADIX, D),
                                      lambda i: (i // steps, 0, i % steps, 0)))
        out_shape.append(jax.ShapeDtypeStruct((T // seq, FOURIER_RADIX, seq // FOURIER_RADIX, D), BF16))
    outs = pl.pallas_call(
        functools.partial(_post_ffn_kernel, emit_next),
        grid=(T // tm,),
        in_specs=[
            pl.BlockSpec((tm, D), row),
            pl.BlockSpec((tm, half), lambda i: (i, a_col)),
            pl.BlockSpec((tm, half), lambda i: (i, b_col)),
            pl.BlockSpec((half, D), lambda i: (0, 0), pipeline_mode=pl.Buffered(1)),
            pl.BlockSpec((half, D), lambda i: (1, 0), pipeline_mode=pl.Buffered(1)),
            _resident((1, D)), _resident((1, D)),
            _resident(wg.shape), _resident(wu.shape), _resident(wd.shape),
            _resident((1, D)), _resident((1, D)), _resident(pick.shape),
        ],
        out_specs=out_specs,
        out_shape=out_shape,
        scratch_shapes=[pltpu.VMEM((tm, dff), BF16)],
        compiler_params=_params("parallel"),
        name="post_ffn",
    )(x, a, b, w_mix, w_mix, vec(g_post), vec(g_pre), wg, wu, wd, vec(g_fpost),
      vec(g_next if emit_next else g_fpost), pick)
    return (outs[0], outs[1]) if emit_next else (outs[0], None)


def kernel(x, ln_mix_pre, ln_mix_post, ln_ffn_pre, ln_ffn_post, w_in, w_mix_out, lambda_q1, lambda_k1,
           lambda_q2, lambda_k2, attn_subln, conv_w, conv_b, lru_w_a, lru_b_a, lru_w_i, lru_b_i, lru_lambda,
           w_fourier_out, w_ffn_gate, w_ffn_up, w_ffn_down):
    B, S, D = x.shape
    T = B * S
    depth = ln_mix_pre.shape[0]
    heads = jnp.arange(1, ATTN_HEADS + 1, dtype=F32)
    slopes = jnp.broadcast_to(jnp.exp2(-8.0 * heads / ATTN_HEADS)[:, None, None], (ATTN_HEADS, 1, HEAD_W))
    xf = x.reshape(T, D)
    h4 = None
    for l in range(depth):
        if l % 2 == 0:
            e = l // 2
            lam_init = 0.8 - 0.6 * float(np.exp(-0.3 * l))
            q, k, v, xr, yg = _inproj(xf, ln_mix_pre[l], w_in[e].astype(BF16))
            width = q.shape[1]
            to_seq = lambda a: a.reshape(B, S, width)
            attn = _attention(to_seq(q), to_seq(k), to_seq(v), slopes, lambda_q1[e], lambda_k1[e],
                              lambda_q2[e], lambda_k2[e], attn_subln[e], lam_init)
            rec = _lru(to_seq(xr), to_seq(yg), conv_w[e], conv_b[e], lru_w_a[e], lru_b_a[e],
                       lru_w_i[e], lru_b_i[e], lru_lambda[e])
            a, a_col = attn.reshape(T, width), 0
            b, b_col = rec.reshape(T, width), 0
            w_mix = w_mix_out[e]
        else:
            f = _fourier(h4).reshape(T, D)
            a, a_col, b, b_col = f, 0, f, 1
            w_mix = w_fourier_out[l // 2]
        next_is_odd = l + 1 < depth and (l + 1) % 2 == 1
        xf, h4 = _post_ffn(xf, a, a_col, b, b_col, w_mix.astype(BF16), ln_mix_post[l], ln_ffn_pre[l],
                           w_ffn_gate[l].astype(BF16), w_ffn_up[l].astype(BF16), w_ffn_down[l].astype(BF16),
                           ln_ffn_post[l], ln_mix_pre[l + 1] if next_is_odd else None, S)
    return xf.reshape(B, S, D)
```

```python
import functools

import numpy as np
import jax
import jax.numpy as jnp
from jax import lax
from jax.experimental import pallas as pl
from jax.experimental.pallas import tpu as pltpu

F32 = jnp.float32
BF16 = jnp.bfloat16

EPS = 1e-6
ATTN_HEADS = 4
QK_DIM = 64
HEAD_W = 2 * QK_DIM
LRU_HEADS = 4
LRU_HEAD_DIM = 128
CONV_WIDTH = 4
LRU_C = 8.0
FOURIER_GROUPS = 4
FOURIER_RADIX = 4

VMEM_LIMIT_BYTES = 54 * 1024 * 1024
POS_SPLIT = 64


def _rms(x, g):
    ms = jnp.mean(x * x, axis=-1, keepdims=True)
    return x * lax.rsqrt(ms + EPS) * g


def _sigmoid(x):
    return 0.5 * jnp.tanh(0.5 * x) + 0.5


def _params(*semantics, flags=None):
    return pltpu.CompilerParams(dimension_semantics=semantics, vmem_limit_bytes=VMEM_LIMIT_BYTES, flags=flags)


def _resident(shape):
    nd = len(shape)
    return pl.BlockSpec(shape, lambda *_: (0,) * nd, pipeline_mode=pl.Buffered(1))


def _inproj_kernel(x_ref, g_ref, w_ref, q_ref, k_ref, v_ref, xr_ref, yg_ref):
    h = _rms(x_ref[...], g_ref[...]).astype(BF16)
    width = q_ref.shape[1]
    for n, o_ref in enumerate((q_ref, k_ref, v_ref, xr_ref, yg_ref)):
        z = jnp.dot(h, w_ref[:, n * width:(n + 1) * width], preferred_element_type=F32)
        if n == 0:
            z = z * (QK_DIM ** -0.5)
        o_ref[...] = z.astype(o_ref.dtype)


def _inproj(x, g, w, tm=512):
    T, D = x.shape
    width = w.shape[1] // 5
    row = lambda i: (i, 0)
    out_shapes = [jax.ShapeDtypeStruct((T, width), dt) for dt in (BF16, BF16, BF16, F32, F32)]
    return pl.pallas_call(
        _inproj_kernel,
        grid=(T // tm,),
        in_specs=[pl.BlockSpec((tm, D), row), _resident((1, D)), _resident(w.shape)],
        out_specs=[pl.BlockSpec((tm, width), row) for _ in out_shapes],
        out_shape=out_shapes,
        compiler_params=_params("parallel"),
        name="inproj",
    )(x, g.reshape(1, D), w)


ATTN_KEY_CHUNK = 256
ATTN_KEY_SECTIONS = 2
ATTN_QUERY_BLOCK = 512
ATTN_BLOCKS_PER_STEP = 4
ATTN_SCORE_SLOTS = 2
BIAS_ROWS = 16
V_ONES_ROWS = 16


def _attn_kernel(lam_init, tq, q_ref, k_ref, v_ref, slope_ref, lq1_ref, lk1_ref, lq2_ref, lk2_ref,
                 g_ref, o_ref, ka_ref, vt_ref, corr_ref, s_ref):
    S = k_ref.shape[1]
    tk = ATTN_KEY_CHUNK
    nchunk = S // tk
    ndiag = tq // tk
    i = pl.program_id(2)
    sig_row = slope_ref[0]
    sig = sig_row[:, 0:1]

    @pl.when(i == 0)
    def _build_keys_values():
        for c in range(nchunk):
            rows = slice(c * tk, (c + 1) * tk)
            kb = k_ref[0, rows, :].astype(F32)
            lane = lax.broadcasted_iota(jnp.int32, (tk, HEAD_W), 1)
            j = lax.broadcasted_iota(jnp.int32, (tk, HEAD_W), 0) + c * tk
            j_lo = j & (POS_SPLIT - 1)
            j_hi = j - j_lo
            aug = jnp.where(lane < 2, 1.0,
                            jnp.where(lane == 2, sig_row * j_lo.astype(F32),
                                      jnp.where(lane == 3, sig_row * j_hi.astype(F32), 0.0)))
            ka_ref[0, rows, 0:HEAD_W] = jnp.where(lane < QK_DIM, kb, 0.0).astype(BF16)
            ka_ref[1, rows, 0:HEAD_W] = jnp.where(lane >= QK_DIM, kb, 0.0).astype(BF16)
            ka_ref[0, rows, HEAD_W:2 * HEAD_W] = aug.astype(BF16)
            ka_ref[1, rows, HEAD_W:2 * HEAD_W] = aug.astype(BF16)
            sub = lax.broadcasted_iota(jnp.int32, (V_ONES_ROWS, tk), 0)
            vt_ref[c, 0:HEAD_W, :] = v_ref[0, rows, :].astype(F32).T.astype(BF16)
            vt_ref[c, HEAD_W:HEAD_W + V_ONES_ROWS, :] = jnp.where(sub == 0, 1.0, 0.0).astype(BF16)
        key_idx = lax.broadcasted_iota(jnp.int32, (tk, tq), 0)
        qry_idx = lax.broadcasted_iota(jnp.int32, (tk, tq), 1)
        for d in range(ndiag):
            rel = key_idx - qry_idx + d * tk
            corr_ref[d] = (-2.0 * sig) * jnp.maximum(rel, 0).astype(F32)

    lam = (jnp.exp(jnp.sum(lq1_ref[...] * lk1_ref[...], axis=1, keepdims=True))
           - jnp.exp(jnp.sum(lq2_ref[...] * lk2_ref[...], axis=1, keepdims=True)) + lam_init)
    zero_rows = jnp.zeros((HEAD_W - BIAS_ROWS, tq), BF16)

    def block_setup(blk):
        first = (i * ATTN_BLOCKS_PER_STEP + blk) * ndiag
        q_t = q_ref[0, blk * tq:(blk + 1) * tq, :].astype(F32).T.astype(BF16)
        sub = lax.broadcasted_iota(jnp.int32, (BIAS_ROWS, tq), 0)
        pos = lax.broadcasted_iota(jnp.int32, (BIAS_ROWS, tq), 1) + first * tk
        p_lo = pos & (POS_SPLIT - 1)
        p_hi = pos - p_lo
        bias_t = jnp.where(sub == 0, -sig * p_lo.astype(F32),
                           jnp.where(sub == 1, -sig * p_hi.astype(F32),
                                     jnp.where(sub < 4, 1.0, 0.0)))
        bias_left = bias_t.astype(BF16)
        bias_right = (-bias_t).astype(BF16)
        chunk_ids, starts, weights = [], [], []
        for r in range(nchunk):
            c = first + r
            wrapped = c >= nchunk
            c = jnp.where(wrapped, c - nchunk, c)
            chunk_ids.append(c)
            starts.append(pl.multiple_of(c * tk, tk))
            bias = bias_left if r < ndiag else jnp.where(wrapped, bias_left, bias_right)
            weights.append(jnp.concatenate([q_t, bias, zero_rows], axis=0))
        return chunk_ids, starts, weights

    def running_max(mx, t):
        return t if mx is None else jnp.maximum(mx, t)

    def unfoldable_zero(x):
        return pltpu.bitcast(lax.shift_right_logical(pltpu.bitcast(x, jnp.uint32), jnp.uint32(32)), F32)

    per_sec = nchunk // ATTN_KEY_SECTIONS
    jobs = [(blk, sec, m) for blk in range(ATTN_BLOCKS_PER_STEP)
            for sec in range(ATTN_KEY_SECTIONS) for m in range(2)]
    setups = {}
    row_max, acc = {}, {}

    def scores(j, c):
        blk, sec, m = jobs[j]
        _, starts, weights = setups[blk]
        r = sec * per_sec + c
        s_t = jnp.dot(ka_ref[m, pl.ds(starts[r], tk), :], weights[r], preferred_element_type=F32)
        if r < ndiag:
            s_t = s_t + corr_ref[r]
        s_ref[j % ATTN_SCORE_SLOTS, c] = s_t
        return jnp.max(s_t.reshape(tk // 8, 8, tq), axis=0)

    def probs(j, c, mrow):
        return jnp.exp((s_ref[j % ATTN_SCORE_SLOTS, c] - mrow).astype(BF16))

    def weighted_values(j, c, p):
        blk, sec, m = jobs[j]
        chunk_ids = setups[blk][0]
        part = jnp.dot(vt_ref[chunk_ids[sec * per_sec + c]], p, preferred_element_type=F32)
        acc[blk, m] = part if (blk, m) not in acc else acc[blk, m] + part

    def start_softmax(j, mx):
        blk, _, m = jobs[j]
        new = jnp.max(mx, axis=0, keepdims=True)
        if (blk, m) in row_max:
            new = jnp.maximum(new, row_max[blk, m])
            acc[blk, m] = acc[blk, m] * jnp.exp(row_max[blk, m] - new)
        row_max[blk, m] = new

    def finish_block(blk):
        out0 = acc[blk, 0][0:HEAD_W] / acc[blk, 0][HEAD_W:HEAD_W + 1]
        out1 = acc[blk, 1][0:HEAD_W] / acc[blk, 1][HEAD_W:HEAD_W + 1]
        o = (out0 - lam * out1).T
        o = _rms(o, g_ref[...]) * (1.0 - lam_init)
        o_ref[0, blk * tq:(blk + 1) * tq, :] = o.astype(o_ref.dtype)

    lag = 2
    jobs_per_block = 2 * ATTN_KEY_SECTIONS
    for j in range(len(jobs) + 1):
        cur = j if j < len(jobs) else None
        prev = j - 1 if j >= 1 else None
        if cur is not None and jobs[cur][0] not in setups:
            setups[jobs[cur][0]] = block_setup(jobs[cur][0])
        mx, pending, pace = None, [], None
        if prev is not None:
            pace = row_max[jobs[prev][0], jobs[prev][2]]
        for c in range(per_sec):
            if cur is not None:
                tile_mx = scores(cur, c)
                mx = running_max(mx, tile_mx)
            if prev is not None:
                pending.append(probs(prev, c, pace))
                if cur is not None:
                    pace = row_max[jobs[prev][0], jobs[prev][2]] + unfoldable_zero(tile_mx[0:1])
                if c >= lag:
                    weighted_values(prev, c - lag, pending[c - lag])
        if prev is not None:
            for c in range(per_sec - lag, per_sec):
                weighted_values(prev, c, pending[c])
        if cur is not None:
            start_softmax(cur, mx)
        if prev is not None and (prev + 1) % jobs_per_block == 0:
            finish_block(jobs[prev][0])


def _attention(q, k, v, slopes, lq1, lk1, lq2, lk2, subln_g, lam_init):
    B, S, W = q.shape
    H = W // HEAD_W
    tk = ATTN_KEY_CHUNK
    tq = ATTN_QUERY_BLOCK
    step_q = tq * ATTN_BLOCKS_PER_STEP
    kernel = functools.partial(_attn_kernel, lam_init, tq)
    vec = lambda a: a.reshape(1, -1).astype(F32)
    small = lambda n: pl.BlockSpec((1, n), lambda b, h, i: (0, 0))
    return pl.pallas_call(
        kernel,
        grid=(B, H, S // step_q),
        in_specs=[
            pl.BlockSpec((1, step_q, HEAD_W), lambda b, h, i: (b, i, h)),
            pl.BlockSpec((1, S, HEAD_W), lambda b, h, i: (b, 0, h)),
            pl.BlockSpec((1, S, HEAD_W), lambda b, h, i: (b, 0, h)),
            pl.BlockSpec((1, 1, HEAD_W), lambda b, h, i: (h, 0, 0)),
            small(QK_DIM), small(QK_DIM), small(QK_DIM), small(QK_DIM), small(HEAD_W),
        ],
        out_specs=pl.BlockSpec((1, step_q, HEAD_W), lambda b, h, i: (b, i, h)),
        out_shape=jax.ShapeDtypeStruct((B, S, W), BF16),
        scratch_shapes=[
            pltpu.VMEM((2, S, 2 * HEAD_W), BF16),
            pltpu.VMEM((S // tk, HEAD_W + V_ONES_ROWS, tk), BF16),
            pltpu.VMEM((tq // tk, tk, tq), F32),
            pltpu.VMEM((ATTN_SCORE_SLOTS, S // tk // ATTN_KEY_SECTIONS, tk, tq), F32),
        ],
        compiler_params=_params("parallel", "parallel", "arbitrary"),
        name="diff_attention",
    )(q, k, v, slopes, vec(lq1), vec(lk1), vec(lq2), vec(lk2), vec(subln_g))


def _scan8(a, b, reverse):
    row = lax.broadcasted_iota(jnp.int32, a.shape, 0)
    for k in (1, 2, 4):
        if reverse:
            keep = row < 8 - k
            shift = 8 - k
        else:
            keep = row >= k
            shift = k
        a_s = jnp.where(keep, pltpu.roll(a, shift, 0), 1.0)
        b_s = jnp.where(keep, pltpu.roll(b, shift, 0), 0.0)
        b = a * b_s + b
        a = a * a_s
    return a, b


def _lru_kernel(xr_ref, yg_ref, cw_ref, cb_ref, wg_ref, bg_ref, lam_ref, o_ref,
                xp_ref, af_ref, uf_ref, ab_ref, ub_ref):
    S = xr_ref.shape[1]
    C = LRU_HEAD_DIM
    tc = 512
    pad = 8
    xp_ref[0:pad, :] = jnp.zeros((pad, C), F32)
    xp_ref[S + pad:S + 2 * pad, :] = jnp.zeros((pad, C), F32)
    xp_ref[pad:S + pad, :] = xr_ref[0]

    lam = lam_ref[0]
    c_log_sig = LRU_C * (jnp.minimum(lam, 0.0) - jnp.log1p(jnp.exp(-jnp.abs(lam))))
    cw = cw_ref[...]
    for c in range(S // tc):
        base = c * tc
        xc = cb_ref[...]
        for t in range(CONV_WIDTH):
            off = base + pad - CONV_WIDTH // 2 + t
            xc = xc + cw[t:t + 1, :] * xp_ref[off:off + tc, :]
        gates = jnp.dot(xc.astype(BF16), wg_ref[0], preferred_element_type=F32) + bg_ref[0]
        for d, (a_ref, u_ref) in enumerate(((af_ref, uf_ref), (ab_ref, ub_ref))):
            r = _sigmoid(gates[:, (2 * d) * C:(2 * d + 1) * C])
            ig = _sigmoid(gates[:, (2 * d + 1) * C:(2 * d + 2) * C])
            log_a = r * c_log_sig[:, d * C:(d + 1) * C]
            a = jnp.exp(log_a)
            one_minus_a2 = -jnp.tanh(log_a) * (1.0 + a * a)
            a_ref[base:base + tc, :] = a
            root = jnp.where(one_minus_a2 > 0.0, one_minus_a2 * lax.rsqrt(one_minus_a2), 0.0)
            u_ref[base:base + tc, :] = root * (ig * xc)

    unroll = 8
    span = 8 * unroll

    def body(it, carry):
        cf, cb = carry
        f0 = pl.multiple_of(it * span, span)
        b0 = pl.multiple_of(S - (it + 1) * span, span)
        af = af_ref[pl.ds(f0, span), :]
        uf = uf_ref[pl.ds(f0, span), :]
        hs = []
        for j in range(unroll):
            a, b = _scan8(af[8 * j:8 * j + 8], uf[8 * j:8 * j + 8], False)
            hs.append(a * cf + b)
            cf = jnp.broadcast_to(a[7:8], (8, C)) * cf + jnp.broadcast_to(b[7:8], (8, C))
        uf_ref[pl.ds(f0, span), :] = jnp.concatenate(hs, axis=0)
        ab = ab_ref[pl.ds(b0, span), :]
        ub = ub_ref[pl.ds(b0, span), :]
        hs = []
        for j in reversed(range(unroll)):
            a, b = _scan8(ab[8 * j:8 * j + 8], ub[8 * j:8 * j + 8], True)
            hs.append(a * cb + b)
            cb = jnp.broadcast_to(a[0:1], (8, C)) * cb + jnp.broadcast_to(b[0:1], (8, C))
        ub_ref[pl.ds(b0, span), :] = jnp.concatenate(hs[::-1], axis=0)
        return cf, cb

    zero = jnp.zeros((8, C), F32)
    lax.fori_loop(0, S // span, body, (zero, zero))

    for c in range(S // tc):
        rows = slice(c * tc, (c + 1) * tc)
        y = yg_ref[0, rows, :]
        gelu = 0.5 * y * (1.0 + jnp.tanh(0.7978845608028654 * (y + 0.044715 * (y * y * y))))
        o_ref[0, rows, :] = ((uf_ref[rows, :] + ub_ref[rows, :]) * gelu).astype(o_ref.dtype)


def _lru(xr, yg, conv_w, conv_b, w_a, b_a, w_i, b_i, lam):
    B, S, W = xr.shape
    C = LRU_HEAD_DIM
    H = W // C
    wg = jnp.concatenate([w_a[0], w_i[0], w_a[1], w_i[1]], axis=-1).astype(BF16)
    per_head = lambda a: a.reshape(H, 1, C)
    bg = jnp.concatenate([per_head(b_a[0]), per_head(b_i[0]), per_head(b_a[1]), per_head(b_i[1])], axis=-1)
    lam2 = jnp.concatenate([per_head(lam[0]), per_head(lam[1])], axis=-1)
    seq = pl.BlockSpec((1, S, C), lambda b, h: (b, 0, h))
    return pl.pallas_call(
        _lru_kernel,
        grid=(B, H),
        in_specs=[
            seq, seq,
            pl.BlockSpec((CONV_WIDTH, C), lambda b, h: (0, h)),
            pl.BlockSpec((1, C), lambda b, h: (0, h)),
            pl.BlockSpec((1, C, 4 * C), lambda b, h: (h, 0, 0)),
            pl.BlockSpec((1, 1, 4 * C), lambda b, h: (h, 0, 0)),
            pl.BlockSpec((1, 1, 2 * C), lambda b, h: (h, 0, 0)),
        ],
        out_specs=seq,
        out_shape=jax.ShapeDtypeStruct((B, S, W), BF16),
        scratch_shapes=[pltpu.VMEM((S + 16, C), F32)] + [pltpu.VMEM((S, C), F32) for _ in range(4)],
        compiler_params=_params("parallel", "parallel"),
        name="rg_lru",
    )(xr, yg, conv_w, conv_b.reshape(1, W), wg, bg, lam2)


@functools.lru_cache(maxsize=None)
def _dft_tables(seq, group_dim):
    n4 = seq // FOURIER_RADIX
    k1 = np.arange(n4, dtype=np.int64)[:, None]
    s1 = np.arange(n4, dtype=np.int64)[None, :]
    cos_t, sin_t = [], []
    for s2 in range(FOURIER_RADIX):
        ang = 2.0 * np.pi * ((k1 * (FOURIER_RADIX * s1 + s2)) % seq).astype(np.float64) / seq
        cos_t.append(np.cos(ang) / np.sqrt(seq))
        sin_t.append(-np.sin(ang) / np.sqrt(seq))
    c = np.arange(group_dim, dtype=np.int64)
    ang = 2.0 * np.pi * ((c[:, None] * c[None, :]) % group_dim).astype(np.float64) / group_dim
    chan = np.concatenate([np.cos(ang), np.sin(ang)], axis=0) / np.sqrt(group_dim)
    return (np.stack(cos_t).astype(np.float32), np.stack(sin_t).astype(np.float32), chan.astype(np.float32))


def _fourier_kernel(h0_ref, h1_ref, h2_ref, h3_ref, fc_ref, fs_ref, cs_ref, o_ref):
    n4 = fc_ref.shape[1]
    rows = 256
    for kc in range(n4 // rows):
        rs = slice(kc * rows, (kc + 1) * rows)
        yr, yi = [], []
        for r, h_ref in enumerate((h0_ref, h1_ref, h2_ref, h3_ref)):
            x = h_ref[0, 0]
            yr.append(jnp.dot(fc_ref[r, rs, :], x, preferred_element_type=F32))
            yi.append(jnp.dot(fs_ref[r, rs, :], x, preferred_element_type=F32))
        xs = (
            (yr[0] + yr[1] + yr[2] + yr[3], yi[0] + yi[1] + yi[2] + yi[3]),
            (yr[0] + yi[1] - yr[2] - yi[3], yi[0] - yr[1] - yi[2] + yr[3]),
            (yr[0] - yr[1] + yr[2] - yr[3], yi[0] - yi[1] + yi[2] - yi[3]),
            (yr[0] - yi[1] - yr[2] + yi[3], yi[0] + yr[1] - yi[2] - yr[3]),
        )
        for k2, (xr, xi) in enumerate(xs):
            z = jnp.concatenate([xr.astype(BF16), xi.astype(BF16)], axis=1)
            f = jnp.dot(z, cs_ref[...], preferred_element_type=F32)
            o_ref[0, k2 * n4 + kc * rows:k2 * n4 + (kc + 1) * rows, :] = f.astype(o_ref.dtype)


def _fourier(h4):
    B, _, n4, D = h4.shape
    S = n4 * FOURIER_RADIX
    G = D // FOURIER_GROUPS
    fc, fs, cs = (jnp.asarray(t).astype(BF16) for t in _dft_tables(S, G))
    per_res = lambda r: pl.BlockSpec((1, 1, n4, G), lambda b, g: (b, r, 0, g))
    return pl.pallas_call(
        _fourier_kernel,
        grid=(B, FOURIER_GROUPS),
        in_specs=[per_res(0), per_res(1), per_res(2), per_res(3),
                  _resident(fc.shape), _resident(fs.shape), _resident(cs.shape)],
        out_specs=pl.BlockSpec((1, S, G), lambda b, g: (b, 0, g)),
        out_shape=jax.ShapeDtypeStruct((B, S, D), BF16),
        compiler_params=_params("parallel", "parallel"),
        name="fourier",
    )(h4, h4, h4, h4, fc, fs, cs)


FFN_ROW_SPLITS = 2
FFN_COL_CHUNK = 256


def _post_ffn_kernel(emit_next, x_ref, a_ref, b_ref, wa_ref, wb_ref, gpost_ref, gpre_ref, wg_ref, wu_ref,
                     wd_ref, gfpost_ref, gnext_ref, pick_ref, o_ref, *rest):
    act_ref = rest[-1]
    tm = x_ref.shape[0]
    dff = wg_ref.shape[1]
    groups = [slice(s * (tm // FFN_ROW_SPLITS), (s + 1) * (tm // FFN_ROW_SPLITS)) for s in range(FFN_ROW_SPLITS)]

    def mix(rows):
        return (jnp.dot(a_ref[rows, :], wa_ref[...], preferred_element_type=F32)
                + jnp.dot(b_ref[rows, :], wb_ref[...], preferred_element_type=F32))

    def gate_up(rows, h):
        for c in range(dff // FFN_COL_CHUNK):
            cols = slice(c * FFN_COL_CHUNK, (c + 1) * FFN_COL_CHUNK)
            g = jnp.dot(h, wg_ref[:, cols], preferred_element_type=F32)
            u = jnp.dot(h, wu_ref[:, cols], preferred_element_type=F32)
            act_ref[rows, cols] = (g * _sigmoid(g) * u).astype(BF16)

    mixed = [mix(rows) for rows in groups]
    xs = []
    for rows, m in zip(groups, mixed):
        x = x_ref[rows, :] + _rms(m, gpost_ref[...])
        xs.append(x)
        gate_up(rows, _rms(x, gpre_ref[...]).astype(BF16))
    for s, (rows, x) in enumerate(zip(groups, xs)):
        y = jnp.dot(act_ref[rows, :], wd_ref[...], preferred_element_type=F32)
        x = x + _rms(y, gfpost_ref[...])
        o_ref[rows, :] = x
        if emit_next:
            hn_ref = rest[0]
            hn = _rms(x, gnext_ref[...]).astype(BF16)
            n = hn.shape[0] // FOURIER_RADIX
            for r in range(FOURIER_RADIX):
                picked = jnp.dot(pick_ref[r], hn, preferred_element_type=F32)
                hn_ref[0, r, s * n:(s + 1) * n, :] = picked.astype(hn_ref.dtype)


def _post_ffn(x, a, a_col, b, b_col, w_mix, g_post, g_pre, wg, wu, wd, g_fpost, g_next, seq, tm=512):
    T, D = x.shape
    half = w_mix.shape[0] // 2
    dff = wg.shape[1]
    emit_next = g_next is not None
    rows_per_group = tm // FFN_ROW_SPLITS
    n_pick = rows_per_group // FOURIER_RADIX
    pick = np.zeros((FOURIER_RADIX, n_pick, rows_per_group), np.float32)
    for r in range(FOURIER_RADIX):
        pick[r, np.arange(n_pick), FOURIER_RADIX * np.arange(n_pick) + r] = 1.0
    pick = jnp.asarray(pick, BF16)
    row = lambda i: (i, 0)
    vec = lambda g: g.reshape(1, D)
    out_specs = [pl.BlockSpec((tm, D), row)]
    out_shape = [jax.ShapeDtypeStruct((T, D), F32)]
    if emit_next:
        steps = seq // tm
        out_specs.append(pl.BlockSpec((1, FOURIER_RADIX, tm // FOURIER_RADIX, D),
                                      lambda i: (i // steps, 0, i % steps, 0)))
        out_shape.append(jax.ShapeDtypeStruct((T // seq, FOURIER_RADIX, seq // FOURIER_RADIX, D), BF16))
    outs = pl.pallas_call(
        functools.partial(_post_ffn_kernel, emit_next),
        grid=(T // tm,),
        in_specs=[
            pl.BlockSpec((tm, D), row),
            pl.BlockSpec((tm, half), lambda i: (i, a_col)),
            pl.BlockSpec((tm, half), lambda i: (i, b_col)),
            pl.BlockSpec((half, D), lambda i: (0, 0), pipeline_mode=pl.Buffered(1)),
            pl.BlockSpec((half, D), lambda i: (1, 0), pipeline_mode=pl.Buffered(1)),
            _resident((1, D)), _resident((1, D)),
            _resident(wg.shape), _resident(wu.shape), _resident(wd.shape),
            _resident((1, D)), _resident((1, D)), _resident(pick.shape),
        ],
        out_specs=out_specs,
        out_shape=out_shape,
        scratch_shapes=[pltpu.VMEM((tm, dff), BF16)],
        compiler_params=_params("parallel"),
        name="post_ffn",
    )(x, a, b, w_mix, w_mix, vec(g_post), vec(g_pre), wg, wu, wd, vec(g_fpost),
      vec(g_next if emit_next else g_fpost), pick)
    return (outs[0], outs[1]) if emit_next else (outs[0], None)


def kernel(x, ln_mix_pre, ln_mix_post, ln_ffn_pre, ln_ffn_post, w_in, w_mix_out, lambda_q1, lambda_k1,
           lambda_q2, lambda_k2, attn_subln, conv_w, conv_b, lru_w_a, lru_b_a, lru_w_i, lru_b_i, lru_lambda,
           w_fourier_out, w_ffn_gate, w_ffn_up, w_ffn_down):
    B, S, D = x.shape
    T = B * S
    depth = ln_mix_pre.shape[0]
    heads = jnp.arange(1, ATTN_HEADS + 1, dtype=F32)
    slopes = jnp.broadcast_to(jnp.exp2(-8.0 * heads / ATTN_HEADS)[:, None, None], (ATTN_HEADS, 1, HEAD_W))
    xf = x.reshape(T, D)
    h4 = None
    for l in range(depth):
        if l % 2 == 0:
            e = l // 2
            lam_init = 0.8 - 0.6 * float(np.exp(-0.3 * l))
            q, k, v, xr, yg = _inproj(xf, ln_mix_pre[l], w_in[e].astype(BF16))
            width = q.shape[1]
            to_seq = lambda a: a.reshape(B, S, width)
            attn = _attention(to_seq(q), to_seq(k), to_seq(v), slopes, lambda_q1[e], lambda_k1[e],
                              lambda_q2[e], lambda_k2[e], attn_subln[e], lam_init)
            rec = _lru(to_seq(xr), to_seq(yg), conv_w[e], conv_b[e], lru_w_a[e], lru_b_a[e],
                       lru_w_i[e], lru_b_i[e], lru_lambda[e])
            a, a_col = attn.reshape(T, width), 0
            b, b_col = rec.reshape(T, width), 0
            w_mix = w_mix_out[e]
        else:
            f = _fourier(h4).reshape(T, D)
            a, a_col, b, b_col = f, 0, f, 1
            w_mix = w_fourier_out[l // 2]
        next_is_odd = l + 1 < depth and (l + 1) % 2 == 1
        xf, h4 = _post_ffn(xf, a, a_col, b, b_col, w_mix.astype(BF16), ln_mix_post[l], ln_ffn_pre[l],
                           w_ffn_gate[l].astype(BF16), w_ffn_up[l].astype(BF16), w_ffn_down[l].astype(BF16),
                           ln_ffn_post[l], ln_mix_pre[l + 1] if next_is_odd else None, S)
    return xf.reshape(B, S, D)
```

```python
import functools

import numpy as np
import jax
import jax.numpy as jnp
from jax import lax
from jax.experimental import pallas as pl
from jax.experimental.pallas import tpu as pltpu

F32 = jnp.float32
BF16 = jnp.bfloat16

EPS = 1e-6
ATTN_HEADS = 4
QK_DIM = 64
HEAD_W = 2 * QK_DIM
LRU_HEADS = 4
LRU_HEAD_DIM = 128
CONV_WIDTH = 4
LRU_C = 8.0
FOURIER_GROUPS = 4
FOURIER_RADIX = 4

VMEM_LIMIT_BYTES = 54 * 1024 * 1024
POS_SPLIT = 64


def _rms(x, g):
    ms = jnp.mean(x * x, axis=-1, keepdims=True)
    return x * lax.rsqrt(ms + EPS) * g


def _sigmoid(x):
    return 0.5 * jnp.tanh(0.5 * x) + 0.5


def _params(*semantics, flags=None):
    return pltpu.CompilerParams(dimension_semantics=semantics, vmem_limit_bytes=VMEM_LIMIT_BYTES, flags=flags)


def _resident(shape):
    nd = len(shape)
    return pl.BlockSpec(shape, lambda *_: (0,) * nd, pipeline_mode=pl.Buffered(1))


def _inproj_kernel(x_ref, g_ref, w_ref, q_ref, k_ref, v_ref, xr_ref, yg_ref):
    h = _rms(x_ref[...], g_ref[...]).astype(BF16)
    width = q_ref.shape[1]
    for n, o_ref in enumerate((q_ref, k_ref, v_ref, xr_ref, yg_ref)):
        z = jnp.dot(h, w_ref[:, n * width:(n + 1) * width], preferred_element_type=F32)
        if n == 0:
            z = z * (QK_DIM ** -0.5)
        o_ref[...] = z.astype(o_ref.dtype)


def _inproj(x, g, w, tm=512):
    T, D = x.shape
    width = w.shape[1] // 5
    row = lambda i: (i, 0)
    out_shapes = [jax.ShapeDtypeStruct((T, width), dt) for dt in (BF16, BF16, BF16, F32, F32)]
    return pl.pallas_call(
        _inproj_kernel,
        grid=(T // tm,),
        in_specs=[pl.BlockSpec((tm, D), row), _resident((1, D)), _resident(w.shape)],
        out_specs=[pl.BlockSpec((tm, width), row) for _ in out_shapes],
        out_shape=out_shapes,
        compiler_params=_params("parallel"),
        name="inproj",
    )(x, g.reshape(1, D), w)


ATTN_KEY_CHUNK = 256
ATTN_KEY_SECTIONS = 2
ATTN_QUERY_BLOCK = 512
ATTN_BLOCKS_PER_STEP = 4
ATTN_SCORE_SLOTS = 2
BIAS_ROWS = 16
V_ONES_ROWS = 16


def _attn_kernel(lam_init, tq, q_ref, k_ref, v_ref, slope_ref, lq1_ref, lk1_ref, lq2_ref, lk2_ref,
                 g_ref, o_ref, ka_ref, vt_ref, corr_ref, s_ref):
    S = k_ref.shape[1]
    tk = ATTN_KEY_CHUNK
    nchunk = S // tk
    ndiag = tq // tk
    i = pl.program_id(2)
    sig_row = slope_ref[0]
    sig = sig_row[:, 0:1]

    @pl.when(i == 0)
    def _build_keys_values():
        for c in range(nchunk):
            rows = slice(c * tk, (c + 1) * tk)
            kb = k_ref[0, rows, :].astype(F32)
            lane = lax.broadcasted_iota(jnp.int32, (tk, HEAD_W), 1)
            j = lax.broadcasted_iota(jnp.int32, (tk, HEAD_W), 0) + c * tk
            j_lo = j & (POS_SPLIT - 1)
            j_hi = j - j_lo
            aug = jnp.where(lane < 2, 1.0,
                            jnp.where(lane == 2, sig_row * j_lo.astype(F32),
                                      jnp.where(lane == 3, sig_row * j_hi.astype(F32), 0.0)))
            ka_ref[0, rows, 0:HEAD_W] = jnp.where(lane < QK_DIM, kb, 0.0).astype(BF16)
            ka_ref[1, rows, 0:HEAD_W] = jnp.where(lane >= QK_DIM, kb, 0.0).astype(BF16)
            ka_ref[0, rows, HEAD_W:2 * HEAD_W] = aug.astype(BF16)
            ka_ref[1, rows, HEAD_W:2 * HEAD_W] = aug.astype(BF16)
            sub = lax.broadcasted_iota(jnp.int32, (V_ONES_ROWS, tk), 0)
            vt_ref[c, 0:HEAD_W, :] = v_ref[0, rows, :].astype(F32).T.astype(BF16)
            vt_ref[c, HEAD_W:HEAD_W + V_ONES_ROWS, :] = jnp.where(sub == 0, 1.0, 0.0).astype(BF16)
        key_idx = lax.broadcasted_iota(jnp.int32, (tk, tq), 0)
        qry_idx = lax.broadcasted_iota(jnp.int32, (tk, tq), 1)
        for d in range(ndiag):
            rel = key_idx - qry_idx + d * tk
            corr_ref[d] = (-2.0 * sig) * jnp.maximum(rel, 0).astype(F32)

    lam = (jnp.exp(jnp.sum(lq1_ref[...] * lk1_ref[...], axis=1, keepdims=True))
           - jnp.exp(jnp.sum(lq2_ref[...] * lk2_ref[...], axis=1, keepdims=True)) + lam_init)
    zero_rows = jnp.zeros((HEAD_W - BIAS_ROWS, tq), BF16)

    def block_setup(blk):
        first = (i * ATTN_BLOCKS_PER_STEP + blk) * ndiag
        q_t = q_ref[0, blk * tq:(blk + 1) * tq, :].astype(F32).T.astype(BF16)
        sub = lax.broadcasted_iota(jnp.int32, (BIAS_ROWS, tq), 0)
        pos = lax.broadcasted_iota(jnp.int32, (BIAS_ROWS, tq), 1) + first * tk
        p_lo = pos & (POS_SPLIT - 1)
        p_hi = pos - p_lo
        bias_t = jnp.where(sub == 0, -sig * p_lo.astype(F32),
                           jnp.where(sub == 1, -sig * p_hi.astype(F32),
                                     jnp.where(sub < 4, 1.0, 0.0)))
        bias_left = bias_t.astype(BF16)
        bias_right = (-bias_t).astype(BF16)
        chunk_ids, starts, weights = [], [], []
        for r in range(nchunk):
            c = first + r
            wrapped = c >= nchunk
            c = jnp.where(wrapped, c - nchunk, c)
            chunk_ids.append(c)
            starts.append(pl.multiple_of(c * tk, tk))
            bias = bias_left if r < ndiag else jnp.where(wrapped, bias_left, bias_right)
            weights.append(jnp.concatenate([q_t, bias, zero_rows], axis=0))
        return chunk_ids, starts, weights

    def running_max(mx, t):
        return t if mx is None else jnp.maximum(mx, t)

    def unfoldable_zero(x):
        return pltpu.bitcast(lax.shift_right_logical(pltpu.bitcast(x, jnp.uint32), jnp.uint32(32)), F32)

    per_sec = nchunk // ATTN_KEY_SECTIONS
    jobs = [(blk, sec, m) for blk in range(ATTN_BLOCKS_PER_STEP)
            for sec in range(ATTN_KEY_SECTIONS) for m in range(2)]
    setups = {}
    row_max, acc = {}, {}

    def scores(j, c):
        blk, sec, m = jobs[j]
        _, starts, weights = setups[blk]
        r = sec * per_sec + c
        s_t = jnp.dot(ka_ref[m, pl.ds(starts[r], tk), :], weights[r], preferred_element_type=F32)
        if r < ndiag:
            s_t = s_t + corr_ref[r]
        s_ref[j % ATTN_SCORE_SLOTS, c] = s_t
        return jnp.max(s_t.reshape(tk // 8, 8, tq), axis=0)

    def probs(j, c, mrow):
        return jnp.exp((s_ref[j % ATTN_SCORE_SLOTS, c] - mrow).astype(BF16))

    def weighted_values(j, c, p):
        blk, sec, m = jobs[j]
        chunk_ids = setups[blk][0]
        part = jnp.dot(vt_ref[chunk_ids[sec * per_sec + c]], p, preferred_element_type=F32)
        acc[blk, m] = part if (blk, m) not in acc else acc[blk, m] + part

    def start_softmax(j, mx):
        blk, _, m = jobs[j]
        new = jnp.max(mx, axis=0, keepdims=True)
        if (blk, m) in row_max:
            new = jnp.maximum(new, row_max[blk, m])
            acc[blk, m] = acc[blk, m] * jnp.exp(row_max[blk, m] - new)
        row_max[blk, m] = new

    def finish_block(blk):
        out0 = acc[blk, 0][0:HEAD_W] / acc[blk, 0][HEAD_W:HEAD_W + 1]
        out1 = acc[blk, 1][0:HEAD_W] / acc[blk, 1][HEAD_W:HEAD_W + 1]
        o = (out0 - lam * out1).T
        o = _rms(o, g_ref[...]) * (1.0 - lam_init)
        o_ref[0, blk * tq:(blk + 1) * tq, :] = o.astype(o_ref.dtype)

    lag = 2
    jobs_per_block = 2 * ATTN_KEY_SECTIONS
    for j in range(len(jobs) + 1):
        cur = j if j < len(jobs) else None
        prev = j - 1 if j >= 1 else None
        if cur is not None and jobs[cur][0] not in setups:
            setups[jobs[cur][0]] = block_setup(jobs[cur][0])
        mx, pending, pace = None, [], None
        if prev is not None:
            pace = row_max[jobs[prev][0], jobs[prev][2]]
        for c in range(per_sec):
            if cur is not None:
                tile_mx = scores(cur, c)
                mx = running_max(mx, tile_mx)
            if prev is not None:
                pending.append(probs(prev, c, pace))
                if cur is not None:
                    pace = row_max[jobs[prev][0], jobs[prev][2]] + unfoldable_zero(tile_mx[0:1])
                if c >= lag:
                    weighted_values(prev, c - lag, pending[c - lag])
        if prev is not None:
            for c in range(per_sec - lag, per_sec):
                weighted_values(prev, c, pending[c])
        if cur is not None:
            start_softmax(cur, mx)
        if prev is not None and (prev + 1) % jobs_per_block == 0:
            finish_block(jobs[prev][0])


def _attention(q, k, v, slopes, lq1, lk1, lq2, lk2, subln_g, lam_init):
    B, S, W = q.shape
    H = W // HEAD_W
    tk = ATTN_KEY_CHUNK
    tq = ATTN_QUERY_BLOCK
    step_q = tq * ATTN_BLOCKS_PER_STEP
    kernel = functools.partial(_attn_kernel, lam_init, tq)
    vec = lambda a: a.reshape(1, -1).astype(F32)
    small = lambda n: pl.BlockSpec((1, n), lambda b, h, i: (0, 0))
    return pl.pallas_call(
        kernel,
        grid=(B, H, S // step_q),
        in_specs=[
            pl.BlockSpec((1, step_q, HEAD_W), lambda b, h, i: (b, i, h)),
            pl.BlockSpec((1, S, HEAD_W), lambda b, h, i: (b, 0, h)),
            pl.BlockSpec((1, S, HEAD_W), lambda b, h, i: (b, 0, h)),
            pl.BlockSpec((1, 1, HEAD_W), lambda b, h, i: (h, 0, 0)),
            small(QK_DIM), small(QK_DIM), small(QK_DIM), small(QK_DIM), small(HEAD_W),
        ],
        out_specs=pl.BlockSpec((1, step_q, HEAD_W), lambda b, h, i: (b, i, h)),
        out_shape=jax.ShapeDtypeStruct((B, S, W), BF16),
        scratch_shapes=[
            pltpu.VMEM((2, S, 2 * HEAD_W), BF16),
            pltpu.VMEM((S // tk, HEAD_W + V_ONES_ROWS, tk), BF16),
            pltpu.VMEM((tq // tk, tk, tq), F32),
            pltpu.VMEM((ATTN_SCORE_SLOTS, S // tk // ATTN_KEY_SECTIONS, tk, tq), F32),
        ],
        compiler_params=_params("parallel", "parallel", "arbitrary"),
        name="diff_attention",
    )(q, k, v, slopes, vec(lq1), vec(lk1), vec(lq2), vec(lk2), vec(subln_g))


def _scan8(a, b, reverse):
    row = lax.broadcasted_iota(jnp.int32, a.shape, 0)
    for k in (1, 2, 4):
        if reverse:
            keep = row < 8 - k
            shift = 8 - k
        else:
            keep = row >= k
            shift = k
        a_s = jnp.where(keep, pltpu.roll(a, shift, 0), 1.0)
        b_s = jnp.where(keep, pltpu.roll(b, shift, 0), 0.0)
        b = a * b_s + b
        a = a * a_s
    return a, b


def _lru_kernel(xr_ref, yg_ref, cw_ref, cb_ref, wg_ref, bg_ref, lam_ref, o_ref,
                xp_ref, af_ref, uf_ref, ab_ref, ub_ref):
    S = xr_ref.shape[1]
    C = LRU_HEAD_DIM
    tc = 512
    pad = 8
    xp_ref[0:pad, :] = jnp.zeros((pad, C), F32)
    xp_ref[S + pad:S + 2 * pad, :] = jnp.zeros((pad, C), F32)
    xp_ref[pad:S + pad, :] = xr_ref[0]

    lam = lam_ref[0]
    c_log_sig = LRU_C * (jnp.minimum(lam, 0.0) - jnp.log1p(jnp.exp(-jnp.abs(lam))))
    cw = cw_ref[...]
    for c in range(S // tc):
        base = c * tc
        xc = cb_ref[...]
        for t in range(CONV_WIDTH):
            off = base + pad - CONV_WIDTH // 2 + t
            xc = xc + cw[t:t + 1, :] * xp_ref[off:off + tc, :]
        gates = jnp.dot(xc.astype(BF16), wg_ref[0], preferred_element_type=F32) + bg_ref[0]
        for d, (a_ref, u_ref) in enumerate(((af_ref, uf_ref), (ab_ref, ub_ref))):
            r = _sigmoid(gates[:, (2 * d) * C:(2 * d + 1) * C])
            ig = _sigmoid(gates[:, (2 * d + 1) * C:(2 * d + 2) * C])
            log_a = r * c_log_sig[:, d * C:(d + 1) * C]
            a = jnp.exp(log_a)
            one_minus_a2 = -jnp.tanh(log_a) * (1.0 + a * a)
            a_ref[base:base + tc, :] = a
            root = jnp.where(one_minus_a2 > 0.0, one_minus_a2 * lax.rsqrt(one_minus_a2), 0.0)
            u_ref[base:base + tc, :] = root * (ig * xc)

    unroll = 8
    span = 8 * unroll

    def body(it, carry):
        cf, cb = carry
        f0 = pl.multiple_of(it * span, span)
        b0 = pl.multiple_of(S - (it + 1) * span, span)
        af = af_ref[pl.ds(f0, span), :]
        uf = uf_ref[pl.ds(f0, span), :]
        hs = []
        for j in range(unroll):
            a, b = _scan8(af[8 * j:8 * j + 8], uf[8 * j:8 * j + 8], False)
            hs.append(a * cf + b)
            cf = jnp.broadcast_to(a[7:8], (8, C)) * cf + jnp.broadcast_to(b[7:8], (8, C))
        uf_ref[pl.ds(f0, span), :] = jnp.concatenate(hs, axis=0)
        ab = ab_ref[pl.ds(b0, span), :]
        ub = ub_ref[pl.ds(b0, span), :]
        hs = []
        for j in reversed(range(unroll)):
            a, b = _scan8(ab[8 * j:8 * j + 8], ub[8 * j:8 * j + 8], True)
            hs.append(a * cb + b)
            cb = jnp.broadcast_to(a[0:1], (8, C)) * cb + jnp.broadcast_to(b[0:1], (8, C))
        ub_ref[pl.ds(b0, span), :] = jnp.concatenate(hs[::-1], axis=0)
        return cf, cb

    zero = jnp.zeros((8, C), F32)
    lax.fori_loop(0, S // span, body, (zero, zero))

    for c in range(S // tc):
        rows = slice(c * tc, (c + 1) * tc)
        y = yg_ref[0, rows, :]
        gelu = 0.5 * y * (1.0 + jnp.tanh(0.7978845608028654 * (y + 0.044715 * (y * y * y))))
        o_ref[0, rows, :] = ((uf_ref[rows, :] + ub_ref[rows, :]) * gelu).astype(o_ref.dtype)


def _lru(xr, yg, conv_w, conv_b, w_a, b_a, w_i, b_i, lam):
    B, S, W = xr.shape
    C = LRU_HEAD_DIM
    H = W // C
    wg = jnp.concatenate([w_a[0], w_i[0], w_a[1], w_i[1]], axis=-1).astype(BF16)
    per_head = lambda a: a.reshape(H, 1, C)
    bg = jnp.concatenate([per_head(b_a[0]), per_head(b_i[0]), per_head(b_a[1]), per_head(b_i[1])], axis=-1)
    lam2 = jnp.concatenate([per_head(lam[0]), per_head(lam[1])], axis=-1)
    seq = pl.BlockSpec((1, S, C), lambda b, h: (b, 0, h))
    return pl.pallas_call(
        _lru_kernel,
        grid=(B, H),
        in_specs=[
            seq, seq,
            pl.BlockSpec((CONV_WIDTH, C), lambda b, h: (0, h)),
            pl.BlockSpec((1, C), lambda b, h: (0, h)),
            pl.BlockSpec((1, C, 4 * C), lambda b, h: (h, 0, 0)),
            pl.BlockSpec((1, 1, 4 * C), lambda b, h: (h, 0, 0)),
            pl.BlockSpec((1, 1, 2 * C), lambda b, h: (h, 0, 0)),
        ],
        out_specs=seq,
        out_shape=jax.ShapeDtypeStruct((B, S, W), BF16),
        scratch_shapes=[pltpu.VMEM((S + 16, C), F32)] + [pltpu.VMEM((S, C), F32) for _ in range(4)],
        compiler_params=_params("parallel", "parallel"),
        name="rg_lru",
    )(xr, yg, conv_w, conv_b.reshape(1, W), wg, bg, lam2)


@functools.lru_cache(maxsize=None)
def _dft_tables(seq, group_dim):
    n4 = seq // FOURIER_RADIX
    k1 = np.arange(n4, dtype=np.int64)[:, None]
    s1 = np.arange(n4, dtype=np.int64)[None, :]
    cos_t, sin_t = [], []
    for s2 in range(FOURIER_RADIX):
        ang = 2.0 * np.pi * ((k1 * (FOURIER_RADIX * s1 + s2)) % seq).astype(np.float64) / seq
        cos_t.append(np.cos(ang) / np.sqrt(seq))
        sin_t.append(-np.sin(ang) / np.sqrt(seq))
    c = np.arange(group_dim, dtype=np.int64)
    ang = 2.0 * np.pi * ((c[:, None] * c[None, :]) % group_dim).astype(np.float64) / group_dim
    chan = np.concatenate([np.cos(ang), np.sin(ang)], axis=0) / np.sqrt(group_dim)
    return (np.stack(cos_t).astype(np.float32), np.stack(sin_t).astype(np.float32), chan.astype(np.float32))


def _fourier_kernel(h0_ref, h1_ref, h2_ref, h3_ref, fc_ref, fs_ref, cs_ref, o_ref):
    n4 = fc_ref.shape[1]
    rows = 256
    nkc = n4 // rows

    def sequence_dft(kc):
        rs = slice(kc * rows, (kc + 1) * rows)
        yr, yi = [], []
        for r, h_ref in enumerate((h0_ref, h1_ref, h2_ref, h3_ref)):
            x = h_ref[0, 0]
            yr.append(jnp.dot(fc_ref[r, rs, :], x, preferred_element_type=F32))
            yi.append(jnp.dot(fs_ref[r, rs, :], x, preferred_element_type=F32))
        xs = (
            (yr[0] + yr[1] + yr[2] + yr[3], yi[0] + yi[1] + yi[2] + yi[3]),
            (yr[0] + yi[1] - yr[2] - yi[3], yi[0] - yr[1] - yi[2] + yr[3]),
            (yr[0] - yr[1] + yr[2] - yr[3], yi[0] - yi[1] + yi[2] - yi[3]),
            (yr[0] - yi[1] - yr[2] + yi[3], yi[0] + yr[1] - yi[2] - yr[3]),
        )
        return [jnp.concatenate([xr.astype(BF16), xi.astype(BF16)], axis=1) for xr, xi in xs]

    def channel_dft(kc, zs):
        for k2, z in enumerate(zs):
            f = jnp.dot(z, cs_ref[...], preferred_element_type=F32)
            o_ref[0, k2 * n4 + kc * rows:k2 * n4 + (kc + 1) * rows, :] = f.astype(o_ref.dtype)

    pending = None
    for kc in range(nkc):
        zs = sequence_dft(kc)
        if pending is not None:
            channel_dft(kc - 1, pending)
        pending = zs
    channel_dft(nkc - 1, pending)


def _fourier(h4):
    B, _, n4, D = h4.shape
    S = n4 * FOURIER_RADIX
    G = D // FOURIER_GROUPS
    fc, fs, cs = (jnp.asarray(t).astype(BF16) for t in _dft_tables(S, G))
    per_res = lambda r: pl.BlockSpec((1, 1, n4, G), lambda b, g: (b, r, 0, g))
    return pl.pallas_call(
        _fourier_kernel,
        grid=(B, FOURIER_GROUPS),
        in_specs=[per_res(0), per_res(1), per_res(2), per_res(3),
                  _resident(fc.shape), _resident(fs.shape), _resident(cs.shape)],
        out_specs=pl.BlockSpec((1, S, G), lambda b, g: (b, 0, g)),
        out_shape=jax.ShapeDtypeStruct((B, S, D), BF16),
        compiler_params=_params("parallel", "parallel"),
        name="fourier",
    )(h4, h4, h4, h4, fc, fs, cs)


FFN_ROW_SPLITS = 2
FFN_COL_CHUNK = 256


def _post_ffn_kernel(emit_next, x_ref, a_ref, b_ref, wa_ref, wb_ref, gpost_ref, gpre_ref, wg_ref, wu_ref,
                     wd_ref, gfpost_ref, gnext_ref, pick_ref, o_ref, *rest):
    act_ref = rest[-1]
    tm = x_ref.shape[0]
    dff = wg_ref.shape[1]
    groups = [slice(s * (tm // FFN_ROW_SPLITS), (s + 1) * (tm // FFN_ROW_SPLITS)) for s in range(FFN_ROW_SPLITS)]

    def mix(rows):
        return (jnp.dot(a_ref[rows, :], wa_ref[...], preferred_element_type=F32)
                + jnp.dot(b_ref[rows, :], wb_ref[...], preferred_element_type=F32))

    def gate_up(rows, h):
        for c in range(dff // FFN_COL_CHUNK):
            cols = slice(c * FFN_COL_CHUNK, (c + 1) * FFN_COL_CHUNK)
            g = jnp.dot(h, wg_ref[:, cols], preferred_element_type=F32)
            u = jnp.dot(h, wu_ref[:, cols], preferred_element_type=F32)
            act_ref[rows, cols] = (g * _sigmoid(g) * u).astype(BF16)

    mixed = [mix(rows) for rows in groups]
    xs = []
    for rows, m in zip(groups, mixed):
        x = x_ref[rows, :] + _rms(m, gpost_ref[...])
        xs.append(x)
        gate_up(rows, _rms(x, gpre_ref[...]).astype(BF16))
    normed = []
    for rows, x in zip(groups, xs):
        y = jnp.dot(act_ref[rows, :], wd_ref[...], preferred_element_type=F32)
        x = x + _rms(y, gfpost_ref[...])
        o_ref[rows, :] = x
        if emit_next:
            normed.append(_rms(x, gnext_ref[...]).astype(BF16))
    for s, hn in enumerate(normed):
        hn_ref = rest[0]
        n = hn.shape[0] // FOURIER_RADIX
        for r in range(FOURIER_RADIX):
            picked = jnp.dot(pick_ref[r], hn, preferred_element_type=F32)
            hn_ref[0, r, s * n:(s + 1) * n, :] = picked.astype(hn_ref.dtype)


def _post_ffn(x, a, a_col, b, b_col, w_mix, g_post, g_pre, wg, wu, wd, g_fpost, g_next, seq, tm=512):
    T, D = x.shape
    half = w_mix.shape[0] // 2
    dff = wg.shape[1]
    emit_next = g_next is not None
    rows_per_group = tm // FFN_ROW_SPLITS
    n_pick = rows_per_group // FOURIER_RADIX
    pick = np.zeros((FOURIER_RADIX, n_pick, rows_per_group), np.float32)
    for r in range(FOURIER_RADIX):
        pick[r, np.arange(n_pick), FOURIER_RADIX * np.arange(n_pick) + r] = 1.0
    pick = jnp.asarray(pick, BF16)
    row = lambda i: (i, 0)
    vec = lambda g: g.reshape(1, D)
    out_specs = [pl.BlockSpec((tm, D), row)]
    out_shape = [jax.ShapeDtypeStruct((T, D), F32)]
    if emit_next:
        steps = seq // tm
        out_specs.append(pl.BlockSpec((1, FOURIER_RADIX, tm // FOURIER_RADIX, D),
                                      lambda i: (i // steps, 0, i % steps, 0)))
        out_shape.append(jax.ShapeDtypeStruct((T // seq, FOURIER_RADIX, seq // FOURIER_RADIX, D), BF16))
    outs = pl.pallas_call(
        functools.partial(_post_ffn_kernel, emit_next),
        grid=(T // tm,),
        in_specs=[
            pl.BlockSpec((tm, D), row),
            pl.BlockSpec((tm, half), lambda i: (i, a_col)),
            pl.BlockSpec((tm, half), lambda i: (i, b_col)),
            pl.BlockSpec((half, D), lambda i: (0, 0), pipeline_mode=pl.Buffered(1)),
            pl.BlockSpec((half, D), lambda i: (1, 0), pipeline_mode=pl.Buffered(1)),
            _resident((1, D)), _resident((1, D)),
            _resident(wg.shape), _resident(wu.shape), _resident(wd.shape),
            _resident((1, D)), _resident((1, D)), _resident(pick.shape),
        ],
        out_specs=out_specs,
        out_shape=out_shape,
        scratch_shapes=[pltpu.VMEM((tm, dff), BF16)],
        compiler_params=_params("parallel"),
        name="post_ffn",
    )(x, a, b, w_mix, w_mix, vec(g_post), vec(g_pre), wg, wu, wd, vec(g_fpost),
      vec(g_next if emit_next else g_fpost), pick)
    return (outs[0], outs[1]) if emit_next else (outs[0], None)


def kernel(x, ln_mix_pre, ln_mix_post, ln_ffn_pre, ln_ffn_post, w_in, w_mix_out, lambda_q1, lambda_k1,
           lambda_q2, lambda_k2, attn_subln, conv_w, conv_b, lru_w_a, lru_b_a, lru_w_i, lru_b_i, lru_lambda,
           w_fourier_out, w_ffn_gate, w_ffn_up, w_ffn_down):
    B, S, D = x.shape
    T = B * S
    depth = ln_mix_pre.shape[0]
    heads = jnp.arange(1, ATTN_HEADS + 1, dtype=F32)
    slopes = jnp.broadcast_to(jnp.exp2(-8.0 * heads / ATTN_HEADS)[:, None, None], (ATTN_HEADS, 1, HEAD_W))
    xf = x.reshape(T, D)
    h4 = None
    for l in range(depth):
        if l % 2 == 0:
            e = l // 2
            lam_init = 0.8 - 0.6 * float(np.exp(-0.3 * l))
            q, k, v, xr, yg = _inproj(xf, ln_mix_pre[l], w_in[e].astype(BF16))
            width = q.shape[1]
            to_seq = lambda a: a.reshape(B, S, width)
            attn = _attention(to_seq(q), to_seq(k), to_seq(v), slopes, lambda_q1[e], lambda_k1[e],
                              lambda_q2[e], lambda_k2[e], attn_subln[e], lam_init)
            rec = _lru(to_seq(xr), to_seq(yg), conv_w[e], conv_b[e], lru_w_a[e], lru_b_a[e],
                       lru_w_i[e], lru_b_i[e], lru_lambda[e])
            a, a_col = attn.reshape(T, width), 0
            b, b_col = rec.reshape(T, width), 0
            w_mix = w_mix_out[e]
        else:
            f = _fourier(h4).reshape(T, D)
            a, a_col, b, b_col = f, 0, f, 1
            w_mix = w_fourier_out[l // 2]
        next_is_odd = l + 1 < depth and (l + 1) % 2 == 1
        xf, h4 = _post_ffn(xf, a, a_col, b, b_col, w_mix.astype(BF16), ln_mix_post[l], ln_ffn_pre[l],
                           w_ffn_gate[l].astype(BF16), w_ffn_up[l].astype(BF16), w_ffn_down[l].astype(BF16),
                           ln_ffn_post[l], ln_mix_pre[l + 1] if next_is_odd else None, S)
    return xf.reshape(B, S, D)
```

```python
import functools

import numpy as np
import jax
import jax.numpy as jnp
from jax import lax
from jax.experimental import pallas as pl
from jax.experimental.pallas import tpu as pltpu

F32 = jnp.float32
BF16 = jnp.bfloat16

EPS = 1e-6
ATTN_HEADS = 4
QK_DIM = 64
HEAD_W = 2 * QK_DIM
LRU_HEADS = 4
LRU_HEAD_DIM = 128
CONV_WIDTH = 4
LRU_C = 8.0
FOURIER_GROUPS = 4
FOURIER_RADIX = 4

VMEM_LIMIT_BYTES = 54 * 1024 * 1024
POS_SPLIT = 64


def _rms(x, g):
    ms = jnp.mean(x * x, axis=-1, keepdims=True)
    return x * lax.rsqrt(ms + EPS) * g


def _params(*semantics, flags=None):
    return pltpu.CompilerParams(dimension_semantics=semantics, vmem_limit_bytes=VMEM_LIMIT_BYTES, flags=flags)


def _resident(shape):
    nd = len(shape)
    return pl.BlockSpec(shape, lambda *_: (0,) * nd, pipeline_mode=pl.Buffered(1))


def _inproj_kernel(x_ref, g_ref, w_ref, q_ref, k_ref, v_ref, xr_ref, yg_ref):
    h = _rms(x_ref[...], g_ref[...]).astype(BF16)
    width = q_ref.shape[1]
    for n, o_ref in enumerate((q_ref, k_ref, v_ref, xr_ref, yg_ref)):
        z = jnp.dot(h, w_ref[:, n * width:(n + 1) * width], preferred_element_type=F32)
        if n == 0:
            z = z * (QK_DIM ** -0.5)
        o_ref[...] = z.astype(o_ref.dtype)


def _inproj(x, g, w, tm=512):
    T, D = x.shape
    width = w.shape[1] // 5
    row = lambda i: (i, 0)
    out_shapes = [jax.ShapeDtypeStruct((T, width), dt) for dt in (BF16, BF16, BF16, F32, F32)]
    return pl.pallas_call(
        _inproj_kernel,
        grid=(T // tm,),
        in_specs=[pl.BlockSpec((tm, D), row), _resident((1, D)), _resident(w.shape)],
        out_specs=[pl.BlockSpec((tm, width), row) for _ in out_shapes],
        out_shape=out_shapes,
        compiler_params=_params("parallel"),
        name="inproj",
    )(x, g.reshape(1, D), w)


ATTN_KEY_CHUNK = 256
ATTN_KEY_SECTIONS = 2
ATTN_QUERY_BLOCK = 512
ATTN_BLOCKS_PER_STEP = 4
ATTN_SCORE_SLOTS = 2
BIAS_ROWS = 16
V_ONES_ROWS = 16


def _attn_kernel(lam_init, tq, q_ref, k_ref, v_ref, slope_ref, lq1_ref, lk1_ref, lq2_ref, lk2_ref,
                 g_ref, o_ref, ka_ref, vt_ref, corr_ref, s_ref):
    S = k_ref.shape[1]
    tk = ATTN_KEY_CHUNK
    nchunk = S // tk
    ndiag = tq // tk
    i = pl.program_id(2)
    sig_row = slope_ref[0]
    sig = sig_row[:, 0:1]

    @pl.when(i == 0)
    def _build_keys_values():
        for c in range(nchunk):
            rows = slice(c * tk, (c + 1) * tk)
            kb = k_ref[0, rows, :].astype(F32)
            lane = lax.broadcasted_iota(jnp.int32, (tk, HEAD_W), 1)
            j = lax.broadcasted_iota(jnp.int32, (tk, HEAD_W), 0) + c * tk
            j_lo = j & (POS_SPLIT - 1)
            j_hi = j - j_lo
            aug = jnp.where(lane < 2, 1.0,
                            jnp.where(lane == 2, sig_row * j_lo.astype(F32),
                                      jnp.where(lane == 3, sig_row * j_hi.astype(F32), 0.0)))
            ka_ref[0, rows, 0:HEAD_W] = jnp.where(lane < QK_DIM, kb, 0.0).astype(BF16)
            ka_ref[1, rows, 0:HEAD_W] = jnp.where(lane >= QK_DIM, kb, 0.0).astype(BF16)
            ka_ref[0, rows, HEAD_W:2 * HEAD_W] = aug.astype(BF16)
            ka_ref[1, rows, HEAD_W:2 * HEAD_W] = aug.astype(BF16)
            sub = lax.broadcasted_iota(jnp.int32, (V_ONES_ROWS, tk), 0)
            vt_ref[c, 0:HEAD_W, :] = v_ref[0, rows, :].astype(F32).T.astype(BF16)
            vt_ref[c, HEAD_W:HEAD_W + V_ONES_ROWS, :] = jnp.where(sub == 0, 1.0, 0.0).astype(BF16)
        key_idx = lax.broadcasted_iota(jnp.int32, (tk, tq), 0)
        qry_idx = lax.broadcasted_iota(jnp.int32, (tk, tq), 1)
        for d in range(ndiag):
            rel = key_idx - qry_idx + d * tk
            corr_ref[d] = (-2.0 * sig) * jnp.maximum(rel, 0).astype(F32)

    lam = (jnp.exp(jnp.sum(lq1_ref[...] * lk1_ref[...], axis=1, keepdims=True))
           - jnp.exp(jnp.sum(lq2_ref[...] * lk2_ref[...], axis=1, keepdims=True)) + lam_init)
    zero_rows = jnp.zeros((HEAD_W - BIAS_ROWS, tq), BF16)

    def block_setup(blk):
        first = (i * ATTN_BLOCKS_PER_STEP + blk) * ndiag
        q_t = q_ref[0, blk * tq:(blk + 1) * tq, :].astype(F32).T.astype(BF16)
        sub = lax.broadcasted_iota(jnp.int32, (BIAS_ROWS, tq), 0)
        pos = lax.broadcasted_iota(jnp.int32, (BIAS_ROWS, tq), 1) + first * tk
        p_lo = pos & (POS_SPLIT - 1)
        p_hi = pos - p_lo
        bias_t = jnp.where(sub == 0, -sig * p_lo.astype(F32),
                           jnp.where(sub == 1, -sig * p_hi.astype(F32),
                                     jnp.where(sub < 4, 1.0, 0.0)))
        bias_left = bias_t.astype(BF16)
        bias_right = (-bias_t).astype(BF16)
        chunk_ids, starts, weights = [], [], []
        for r in range(nchunk):
            c = first + r
            wrapped = c >= nchunk
            c = jnp.where(wrapped, c - nchunk, c)
            chunk_ids.append(c)
            starts.append(pl.multiple_of(c * tk, tk))
            bias = bias_left if r < ndiag else jnp.where(wrapped, bias_left, bias_right)
            weights.append(jnp.concatenate([q_t, bias, zero_rows], axis=0))
        return chunk_ids, starts, weights

    def running_max(mx, t):
        return t if mx is None else jnp.maximum(mx, t)

    def unfoldable_zero(x):
        return pltpu.bitcast(lax.shift_right_logical(pltpu.bitcast(x, jnp.uint32), jnp.uint32(32)), F32)

    per_sec = nchunk // ATTN_KEY_SECTIONS
    jobs = [(blk, sec, m) for blk in range(ATTN_BLOCKS_PER_STEP)
            for sec in range(ATTN_KEY_SECTIONS) for m in range(2)]
    setups = {}
    row_max, acc = {}, {}

    def scores(j, c):
        blk, sec, m = jobs[j]
        _, starts, weights = setups[blk]
        r = sec * per_sec + c
        s_t = jnp.dot(ka_ref[m, pl.ds(starts[r], tk), :], weights[r], preferred_element_type=F32)
        if r < ndiag:
            s_t = s_t + corr_ref[r]
        s_ref[j % ATTN_SCORE_SLOTS, c] = s_t
        return jnp.max(s_t.reshape(tk // 8, 8, tq), axis=0)

    def probs(j, c, mrow):
        return jnp.exp((s_ref[j % ATTN_SCORE_SLOTS, c] - mrow).astype(BF16))

    def weighted_values(j, c, p):
        blk, sec, m = jobs[j]
        chunk_ids = setups[blk][0]
        part = jnp.dot(vt_ref[chunk_ids[sec * per_sec + c]], p, preferred_element_type=F32)
        acc[blk, m] = part if (blk, m) not in acc else acc[blk, m] + part

    def start_softmax(j, mx):
        blk, _, m = jobs[j]
        new = jnp.max(mx, axis=0, keepdims=True)
        if (blk, m) in row_max:
            new = jnp.maximum(new, row_max[blk, m])
            acc[blk, m] = acc[blk, m] * jnp.exp(row_max[blk, m] - new)
        row_max[blk, m] = new

    def finish_block(blk):
        out0 = acc[blk, 0][0:HEAD_W] / acc[blk, 0][HEAD_W:HEAD_W + 1]
        out1 = acc[blk, 1][0:HEAD_W] / acc[blk, 1][HEAD_W:HEAD_W + 1]
        o = (out0 - lam * out1).T
        o = _rms(o, g_ref[...]) * (1.0 - lam_init)
        o_ref[0, blk * tq:(blk + 1) * tq, :] = o.astype(o_ref.dtype)

    lag = 2
    jobs_per_block = 2 * ATTN_KEY_SECTIONS
    for j in range(len(jobs) + 1):
        cur = j if j < len(jobs) else None
        prev = j - 1 if j >= 1 else None
        if cur is not None and jobs[cur][0] not in setups:
            setups[jobs[cur][0]] = block_setup(jobs[cur][0])
        mx, pending, pace = None, [], None
        if prev is not None:
            pace = row_max[jobs[prev][0], jobs[prev][2]]
        for c in range(per_sec):
            if cur is not None:
                tile_mx = scores(cur, c)
                mx = running_max(mx, tile_mx)
            if prev is not None:
                pending.append(probs(prev, c, pace))
                if cur is not None:
                    pace = row_max[jobs[prev][0], jobs[prev][2]] + unfoldable_zero(tile_mx[0:1])
                if c >= lag:
                    weighted_values(prev, c - lag, pending[c - lag])
        if prev is not None:
            for c in range(per_sec - lag, per_sec):
                weighted_values(prev, c, pending[c])
        if cur is not None:
            start_softmax(cur, mx)
        if prev is not None and (prev + 1) % jobs_per_block == 0:
            finish_block(jobs[prev][0])


def _attention(q, k, v, slopes, lq1, lk1, lq2, lk2, subln_g, lam_init):
    B, S, W = q.shape
    H = W // HEAD_W
    tk = ATTN_KEY_CHUNK
    tq = ATTN_QUERY_BLOCK
    step_q = tq * ATTN_BLOCKS_PER_STEP
    kernel = functools.partial(_attn_kernel, lam_init, tq)
    vec = lambda a: a.reshape(1, -1).astype(F32)
    small = lambda n: pl.BlockSpec((1, n), lambda b, h, i: (0, 0))
    return pl.pallas_call(
        kernel,
        grid=(B, H, S // step_q),
        in_specs=[
            pl.BlockSpec((1, step_q, HEAD_W), lambda b, h, i: (b, i, h)),
            pl.BlockSpec((1, S, HEAD_W), lambda b, h, i: (b, 0, h)),
            pl.BlockSpec((1, S, HEAD_W), lambda b, h, i: (b, 0, h)),
            pl.BlockSpec((1, 1, HEAD_W), lambda b, h, i: (h, 0, 0)),
            small(QK_DIM), small(QK_DIM), small(QK_DIM), small(QK_DIM), small(HEAD_W),
        ],
        out_specs=pl.BlockSpec((1, step_q, HEAD_W), lambda b, h, i: (b, i, h)),
        out_shape=jax.ShapeDtypeStruct((B, S, W), BF16),
        scratch_shapes=[
            pltpu.VMEM((2, S, 2 * HEAD_W), BF16),
            pltpu.VMEM((S // tk, HEAD_W + V_ONES_ROWS, tk), BF16),
            pltpu.VMEM((tq // tk, tk, tq), F32),
            pltpu.VMEM((ATTN_SCORE_SLOTS, S // tk // ATTN_KEY_SECTIONS, tk, tq), F32),
        ],
        compiler_params=_params("parallel", "parallel", "arbitrary"),
        name="diff_attention",
    )(q, k, v, slopes, vec(lq1), vec(lk1), vec(lq2), vec(lk2), vec(subln_g))


def _scan8(a, b, reverse):
    row = lax.broadcasted_iota(jnp.int32, a.shape, 0)
    for k in (1, 2, 4):
        if reverse:
            keep = row < 8 - k
            shift = 8 - k
        else:
            keep = row >= k
            shift = k
        a_s = jnp.where(keep, pltpu.roll(a, shift, 0), 1.0)
        b_s = jnp.where(keep, pltpu.roll(b, shift, 0), 0.0)
        b = a * b_s + b
        a = a * a_s
    return a, b


GELU_C =0.7978845608028654
GELU_CUBIC = 0.044715


def _lru_kernel(xr_ref, yg_ref, cw_ref, cb_ref, wg_ref, lam_ref, o_ref,
                xp_ref, af_ref, uf_ref, ab_ref, ub_ref):
    S = xr_ref.shape[1]
    C = LRU_HEAD_DIM
    tc = 512
    pad = 8
    xp_ref[0:pad, :] = jnp.zeros((pad, C), F32)
    xp_ref[S + pad:S + 2 * pad, :] = jnp.zeros((pad, C), F32)
    xp_ref[pad:S + pad, :] = xr_ref[0]

    lam = lam_ref[0]
    half_cls = (0.5 * LRU_C) * (jnp.minimum(lam, 0.0) - jnp.log1p(jnp.exp(-jnp.abs(lam))))
    cw = cw_ref[...]
    lane = lax.broadcasted_iota(jnp.int32, (tc, C), 1)
    ones_lanes = jnp.where(lane < 2, 1.0, 0.0).astype(BF16)
    for c in range(S // tc):
        base = c * tc
        xc = cb_ref[...]
        for t in range(CONV_WIDTH):
            off = base + pad - CONV_WIDTH // 2 + t
            xc = xc + cw[t:t + 1, :] * xp_ref[off:off + tc, :]
        lhs = jnp.concatenate([xc.astype(BF16), ones_lanes], axis=1)
        t_gates = jnp.tanh(jnp.dot(lhs, wg_ref[0], preferred_element_type=F32))
        half_x = 0.5 * xc
        for d, (a_ref, u_ref) in enumerate(((af_ref, uf_ref), (ab_ref, ub_ref))):
            t_r = t_gates[:, (2 * d) * C:(2 * d + 1) * C]
            t_i = t_gates[:, (2 * d + 1) * C:(2 * d + 2) * C]
            hc = half_cls[:, d * C:(d + 1) * C]
            log_a = t_r * hc + hc
            a = jnp.exp(log_a)
            one_minus_a2 = jnp.tanh(log_a) * (-1.0 - a * a)
            a_ref[base:base + tc, :] = a
            root = jnp.where(one_minus_a2 > 0.0, one_minus_a2 * lax.rsqrt(one_minus_a2), 0.0)
            u_ref[base:base + tc, :] = root * (t_i * half_x + half_x)

    unroll = 8
    span = 8 * unroll

    def body(it, carry):
        cf, cb = carry
        f0 = pl.multiple_of(it * span, span)
        b0 = pl.multiple_of(S - (it + 1) * span, span)
        af = af_ref[pl.ds(f0, span), :]
        uf = uf_ref[pl.ds(f0, span), :]
        hs = []
        for j in range(unroll):
            a, b = _scan8(af[8 * j:8 * j + 8], uf[8 * j:8 * j + 8], False)
            hs.append(a * cf + b)
            cf = jnp.broadcast_to(a[7:8], (8, C)) * cf + jnp.broadcast_to(b[7:8], (8, C))
        uf_ref[pl.ds(f0, span), :] = jnp.concatenate(hs, axis=0)
        ab = ab_ref[pl.ds(b0, span), :]
        ub = ub_ref[pl.ds(b0, span), :]
        hs = []
        for j in reversed(range(unroll)):
            a, b = _scan8(ab[8 * j:8 * j + 8], ub[8 * j:8 * j + 8], True)
            hs.append(a * cb + b)
            cb = jnp.broadcast_to(a[0:1], (8, C)) * cb + jnp.broadcast_to(b[0:1], (8, C))
        ub_ref[pl.ds(b0, span), :] = jnp.concatenate(hs[::-1], axis=0)
        return cf, cb

    zero = jnp.zeros((8, C), F32)
    lax.fori_loop(0, S // span, body, (zero, zero))

    for c in range(S // tc):
        rows = slice(c * tc, (c + 1) * tc)
        y = yg_ref[0, rows, :]
        half_y = 0.5 * y
        gelu = half_y * jnp.tanh(y * ((y * y) * (GELU_C * GELU_CUBIC) + GELU_C)) + half_y
        o_ref[0, rows, :] = ((uf_ref[rows, :] + ub_ref[rows, :]) * gelu).astype(o_ref.dtype)


def _lru(xr, yg, conv_w, conv_b, w_a, b_a, w_i, b_i, lam):
    B, S, W = xr.shape
    C = LRU_HEAD_DIM
    H = W // C
    per_head = lambda a: a.reshape(H, 1, C)
    wg = 0.5 * jnp.concatenate([w_a[0], w_i[0], w_a[1], w_i[1]], axis=-1)
    bg = 0.5 * jnp.concatenate([per_head(b_a[0]), per_head(b_i[0]), per_head(b_a[1]), per_head(b_i[1])], axis=-1)
    bg_hi = bg.astype(BF16)
    bg_lo = (bg - bg_hi.astype(F32)).astype(BF16)
    wg = jnp.concatenate([wg.astype(BF16), bg_hi, bg_lo, jnp.zeros((H, C - 2, 4 * C), BF16)], axis=1)
    lam2 = jnp.concatenate([per_head(lam[0]), per_head(lam[1])], axis=-1)
    seq = pl.BlockSpec((1, S, C), lambda b, h: (b, 0, h))
    return pl.pallas_call(
        _lru_kernel,
        grid=(B, H),
        in_specs=[
            seq, seq,
            pl.BlockSpec((CONV_WIDTH, C), lambda b, h: (0, h)),
            pl.BlockSpec((1, C), lambda b, h: (0, h)),
            pl.BlockSpec((1, 2 * C, 4 * C), lambda b, h: (h, 0, 0)),
            pl.BlockSpec((1, 1, 2 * C), lambda b, h: (h, 0, 0)),
        ],
        out_specs=seq,
        out_shape=jax.ShapeDtypeStruct((B, S, W), BF16),
        scratch_shapes=[pltpu.VMEM((S + 16, C), F32)] + [pltpu.VMEM((S, C), F32) for _ in range(4)],
        compiler_params=_params("parallel", "parallel"),
        name="rg_lru",
    )(xr, yg, conv_w, conv_b.reshape(1, W), wg, lam2)


@functools.lru_cache(maxsize=None)
def _dft_tables(seq, group_dim):
    n4 = seq // FOURIER_RADIX
    k1 = np.arange(n4, dtype=np.int64)[:, None]
    s1 = np.arange(n4, dtype=np.int64)[None, :]
    cos_t, sin_t = [], []
    for s2 in range(FOURIER_RADIX):
        ang = 2.0 * np.pi * ((k1 * (FOURIER_RADIX * s1 + s2)) % seq).astype(np.float64) / seq
        cos_t.append(np.cos(ang) / np.sqrt(seq))
        sin_t.append(-np.sin(ang) / np.sqrt(seq))
    c = np.arange(group_dim, dtype=np.int64)
    ang = 2.0 * np.pi * ((c[:, None] * c[None, :]) % group_dim).astype(np.float64) / group_dim
    chan = np.concatenate([np.cos(ang), np.sin(ang)], axis=0) / np.sqrt(group_dim)
    return (np.stack(cos_t).astype(np.float32), np.stack(sin_t).astype(np.float32), chan.astype(np.float32))


def _fourier_kernel(h0_ref, h1_ref, h2_ref, h3_ref, fc_ref, fs_ref, cs_ref, o_ref):
    n4 = fc_ref.shape[1]
    rows = 256
    nkc = n4 // rows

    def sequence_dft(kc):
        rs = slice(kc * rows, (kc + 1) * rows)
        yr, yi = [], []
        for r, h_ref in enumerate((h0_ref, h1_ref, h2_ref, h3_ref)):
            x = h_ref[0, 0]
            yr.append(jnp.dot(fc_ref[r, rs, :], x, preferred_element_type=F32))
            yi.append(jnp.dot(fs_ref[r, rs, :], x, preferred_element_type=F32))
        xs = (
            (yr[0] + yr[1] + yr[2] + yr[3], yi[0] + yi[1] + yi[2] + yi[3]),
            (yr[0] + yi[1] - yr[2] - yi[3], yi[0] - yr[1] - yi[2] + yr[3]),
            (yr[0] - yr[1] + yr[2] - yr[3], yi[0] - yi[1] + yi[2] - yi[3]),
            (yr[0] - yi[1] - yr[2] + yi[3], yi[0] + yr[1] - yi[2] - yr[3]),
        )
        return [jnp.concatenate([xr.astype(BF16), xi.astype(BF16)], axis=1) for xr, xi in xs]

    def channel_dft(kc, zs):
        for k2, z in enumerate(zs):
            f = jnp.dot(z, cs_ref[...], preferred_element_type=F32)
            o_ref[0, k2 * n4 + kc * rows:k2 * n4 + (kc + 1) * rows, :] = f.astype(o_ref.dtype)

    pending = None
    for kc in range(nkc):
        zs = sequence_dft(kc)
        if pending is not None:
            channel_dft(kc - 1, pending)
        pending = zs
    channel_dft(nkc - 1, pending)


def _fourier(h4):
    B, _, n4, D = h4.shape
    S = n4 * FOURIER_RADIX
    G = D // FOURIER_GROUPS
    fc, fs, cs = (jnp.asarray(t).astype(BF16) for t in _dft_tables(S, G))
    per_res = lambda r: pl.BlockSpec((1, 1, n4, G), lambda b, g: (b, r, 0, g))
    return pl.pallas_call(
        _fourier_kernel,
        grid=(B, FOURIER_GROUPS),
        in_specs=[per_res(0), per_res(1), per_res(2), per_res(3),
                  _resident(fc.shape), _resident(fs.shape), _resident(cs.shape)],
        out_specs=pl.BlockSpec((1, S, G), lambda b, g: (b, 0, g)),
        out_shape=jax.ShapeDtypeStruct((B, S, D), BF16),
        compiler_params=_params("parallel", "parallel"),
        name="fourier",
    )(h4, h4, h4, h4, fc, fs, cs)


FFN_ROW_SPLITS = 2
FFN_COL_CHUNK = 256


def _post_ffn_kernel(emit_next, x_ref, a_ref, b_ref, wa_ref, wb_ref, gpost_ref, gpre_ref, wg_ref, wu_ref,
                     wd_ref, gfpost_ref, gnext_ref, pick_ref, o_ref, *rest):
    act_ref = rest[-1]
    tm = x_ref.shape[0]
    dff = wg_ref.shape[1]
    groups = [slice(s * (tm // FFN_ROW_SPLITS), (s + 1) * (tm // FFN_ROW_SPLITS)) for s in range(FFN_ROW_SPLITS)]

    def mix(rows):
        return (jnp.dot(a_ref[rows, :], wa_ref[...], preferred_element_type=F32)
                + jnp.dot(b_ref[rows, :], wb_ref[...], preferred_element_type=F32))

    def gate_up(rows, h):
        for c in range(dff // FFN_COL_CHUNK):
            cols = slice(c * FFN_COL_CHUNK, (c + 1) * FFN_COL_CHUNK)
            g = jnp.dot(h, wg_ref[:, cols], preferred_element_type=F32)
            u = jnp.dot(h, wu_ref[:, cols], preferred_element_type=F32)
            half_g = 0.5 * g
            act_ref[rows, cols] = ((half_g * jnp.tanh(half_g) + half_g) * u).astype(BF16)

    mixed = [mix(rows) for rows in groups]
    xs = []
    for rows, m in zip(groups, mixed):
        x = x_ref[rows, :] + _rms(m, gpost_ref[...])
        xs.append(x)
        gate_up(rows, _rms(x, gpre_ref[...]).astype(BF16))
    normed = []
    for rows, x in zip(groups, xs):
        y = jnp.dot(act_ref[rows, :], wd_ref[...], preferred_element_type=F32)
        x = x + _rms(y, gfpost_ref[...])
        o_ref[rows, :] = x
        if emit_next:
            normed.append(_rms(x, gnext_ref[...]).astype(BF16))
    for s, hn in enumerate(normed):
        hn_ref = rest[0]
        n = hn.shape[0] // FOURIER_RADIX
        for r in range(FOURIER_RADIX):
            picked = jnp.dot(pick_ref[r], hn, preferred_element_type=F32)
            hn_ref[0, r, s * n:(s + 1) * n, :] = picked.astype(hn_ref.dtype)


def _post_ffn(x, a, a_col, b, b_col, w_mix, g_post, g_pre, wg, wu, wd, g_fpost, g_next, seq, tm=512):
    T, D = x.shape
    half = w_mix.shape[0] // 2
    dff = wg.shape[1]
    emit_next = g_next is not None
    rows_per_group = tm // FFN_ROW_SPLITS
    n_pick = rows_per_group // FOURIER_RADIX
    pick = np.zeros((FOURIER_RADIX, n_pick, rows_per_group), np.float32)
    for r in range(FOURIER_RADIX):
        pick[r, np.arange(n_pick), FOURIER_RADIX * np.arange(n_pick) + r] = 1.0
    pick = jnp.asarray(pick, BF16)
    row = lambda i: (i, 0)
    vec = lambda g: g.reshape(1, D)
    out_specs = [pl.BlockSpec((tm, D), row)]
    out_shape = [jax.ShapeDtypeStruct((T, D), F32)]
    if emit_next:
        steps = seq // tm
        out_specs.append(pl.BlockSpec((1, FOURIER_RADIX, tm // FOURIER_RADIX, D),
                                      lambda i: (i // steps, 0, i % steps, 0)))
        out_shape.append(jax.ShapeDtypeStruct((T // seq, FOURIER_RADIX, seq // FOURIER_RADIX, D), BF16))
    outs = pl.pallas_call(
        functools.partial(_post_ffn_kernel, emit_next),
        grid=(T // tm,),
        in_specs=[
            pl.BlockSpec((tm, D), row),
            pl.BlockSpec((tm, half), lambda i: (i, a_col)),
            pl.BlockSpec((tm, half), lambda i: (i, b_col)),
            pl.BlockSpec((half, D), lambda i: (0, 0), pipeline_mode=pl.Buffered(1)),
            pl.BlockSpec((half, D), lambda i: (1, 0), pipeline_mode=pl.Buffered(1)),
            _resident((1, D)), _resident((1, D)),
            _resident(wg.shape), _resident(wu.shape), _resident(wd.shape),
            _resident((1, D)), _resident((1, D)), _resident(pick.shape),
        ],
        out_specs=out_specs,
        out_shape=out_shape,
        scratch_shapes=[pltpu.VMEM((tm, dff), BF16)],
        compiler_params=_params("parallel"),
        name="post_ffn",
    )(x, a, b, w_mix, w_mix, vec(g_post), vec(g_pre), wg, wu, wd, vec(g_fpost),
      vec(g_next if emit_next else g_fpost), pick)
    return (outs[0], outs[1]) if emit_next else (outs[0], None)


def kernel(x, ln_mix_pre, ln_mix_post, ln_ffn_pre, ln_ffn_post, w_in, w_mix_out, lambda_q1, lambda_k1,
           lambda_q2, lambda_k2, attn_subln, conv_w, conv_b, lru_w_a, lru_b_a, lru_w_i, lru_b_i, lru_lambda,
           w_fourier_out, w_ffn_gate, w_ffn_up, w_ffn_down):
    B, S, D = x.shape
    T = B * S
    depth = ln_mix_pre.shape[0]
    heads = jnp.arange(1, ATTN_HEADS + 1, dtype=F32)
    slopes = jnp.broadcast_to(jnp.exp2(-8.0 * heads / ATTN_HEADS)[:, None, None], (ATTN_HEADS, 1, HEAD_W))
    xf = x.reshape(T, D)
    h4 = None
    for l in range(depth):
        if l % 2 == 0:
            e = l // 2
            lam_init = 0.8 - 0.6 * float(np.exp(-0.3 * l))
            q, k, v, xr, yg = _inproj(xf, ln_mix_pre[l], w_in[e].astype(BF16))
            width = q.shape[1]
            to_seq = lambda a: a.reshape(B, S, width)
            attn = _attention(to_seq(q), to_seq(k), to_seq(v), slopes, lambda_q1[e], lambda_k1[e],
                              lambda_q2[e], lambda_k2[e], attn_subln[e], lam_init)
            rec = _lru(to_seq(xr), to_seq(yg), conv_w[e], conv_b[e], lru_w_a[e], lru_b_a[e],
                       lru_w_i[e], lru_b_i[e], lru_lambda[e])
            a, a_col = attn.reshape(T, width), 0
            b, b_col = rec.reshape(T, width), 0
            w_mix = w_mix_out[e]
        else:
            f = _fourier(h4).reshape(T, D)
            a, a_col, b, b_col = f, 0, f, 1
            w_mix = w_fourier_out[l // 2]
        next_is_odd = l + 1 < depth and (l + 1) % 2 == 1
        xf, h4 = _post_ffn(xf, a, a_col, b, b_col, w_mix.astype(BF16), ln_mix_post[l], ln_ffn_pre[l],
                           w_ffn_gate[l].astype(BF16), w_ffn_up[l].astype(BF16), w_ffn_down[l].astype(BF16),
                           ln_ffn_post[l], ln_mix_pre[l + 1] if next_is_odd else None, S)
    return xf.reshape(B, S, D)
```

```python
import functools

import numpy as np
import jax
import jax.numpy as jnp
from jax import lax
from jax.experimental import pallas as pl
from jax.experimental.pallas import tpu as pltpu

F32 = jnp.float32
BF16 = jnp.bfloat16

EPS = 1e-6
ATTN_HEADS = 4
QK_DIM = 64
HEAD_W = 2 * QK_DIM
LRU_HEADS = 4
LRU_HEAD_DIM = 128
CONV_WIDTH = 4
LRU_C = 8.0
FOURIER_GROUPS = 4
FOURIER_RADIX = 4

VMEM_LIMIT_BYTES = 54 * 1024 * 1024
POS_SPLIT = 64


def _rms(x, g):
    ms = jnp.mean(x * x, axis=-1, keepdims=True)
    return x * lax.rsqrt(ms + EPS) * g


def _params(*semantics, flags=None):
    return pltpu.CompilerParams(dimension_semantics=semantics, vmem_limit_bytes=VMEM_LIMIT_BYTES, flags=flags)


def _resident(shape):
    nd = len(shape)
    return pl.BlockSpec(shape, lambda *_: (0,) * nd, pipeline_mode=pl.Buffered(1))


def _inproj_kernel(x_ref, g_ref, w_ref, q_ref, k_ref, v_ref, xr_ref, yg_ref):
    h = _rms(x_ref[...], g_ref[...]).astype(BF16)
    width = q_ref.shape[1]
    for n, o_ref in enumerate((q_ref, k_ref, v_ref, xr_ref, yg_ref)):
        z = jnp.dot(h, w_ref[:, n * width:(n + 1) * width], preferred_element_type=F32)
        if n == 0:
            z = z * (QK_DIM ** -0.5)
        o_ref[...] = z.astype(o_ref.dtype)


def _inproj(x, g, w, tm=512):
    T, D = x.shape
    width = w.shape[1] // 5
    row = lambda i: (i, 0)
    out_shapes = [jax.ShapeDtypeStruct((T, width), dt) for dt in (BF16, BF16, BF16, F32, F32)]
    return pl.pallas_call(
        _inproj_kernel,
        grid=(T // tm,),
        in_specs=[pl.BlockSpec((tm, D), row), _resident((1, D)), _resident(w.shape)],
        out_specs=[pl.BlockSpec((tm, width), row) for _ in out_shapes],
        out_shape=out_shapes,
        compiler_params=_params("parallel"),
        name="inproj",
    )(x, g.reshape(1, D), w)


ATTN_KEY_CHUNK = 256
ATTN_QUERY_BLOCK = 512
ATTN_BLOCKS_PER_STEP = 4
ATTN_VALUE_LAG = 2
BIAS_ROWS = 16
V_ONES_ROWS = 16
ATTN_SAFE_SHIFT = 30.0
NORM_SLACK = 1.0 + 2.0 ** -5


def _attn_kernel(lam_init, tq, q_ref, k_ref, v_ref, slope_ref, lq1_ref, lk1_ref, lq2_ref, lk2_ref,
                 g_ref, o_ref, ka_ref, vt_ref, corr_ref, k2_ref, shift_ref):
    S = k_ref.shape[1]
    tk = ATTN_KEY_CHUNK
    nchunk = S // tk
    ndiag = tq // tk
    nblk = ATTN_BLOCKS_PER_STEP
    i = pl.program_id(2)
    sig_row = slope_ref[0]
    sig = sig_row[:, 0:1]

    @pl.when(i == 0)
    def _build_keys_values():
        row = lax.broadcasted_iota(jnp.int32, (HEAD_W, HEAD_W), 0)
        col = lax.broadcasted_iota(jnp.int32, (HEAD_W, HEAD_W), 1)
        per_map = jnp.where(col == 0, jnp.where(row < QK_DIM, 1.0, 0.0),
                            jnp.where(col == 1, jnp.where(row >= QK_DIM, 1.0, 0.0), 0.0)).astype(BF16)
        k2_max = None
        for c in range(nchunk):
            rows = slice(c * tk, (c + 1) * tk)
            kb = k_ref[0, rows, :]
            lane = lax.broadcasted_iota(jnp.int32, (tk, HEAD_W), 1)
            j = lax.broadcasted_iota(jnp.int32, (tk, HEAD_W), 0) + c * tk
            j_lo = j & (POS_SPLIT - 1)
            j_hi = j - j_lo
            aug = jnp.where(lane < 2, 1.0,
                            jnp.where(lane == 2, sig_row * j_lo.astype(F32),
                                      jnp.where(lane == 3, sig_row * j_hi.astype(F32),
                                                jnp.where(lane < 6, 1.0, 0.0))))
            ka_ref[rows, 0:HEAD_W] = kb
            ka_ref[rows, HEAD_W:2 * HEAD_W] = aug.astype(BF16)
            kf = kb.astype(F32)
            norms2 = jnp.dot((kf * kf).astype(BF16), per_map, preferred_element_type=F32)
            tile_max = jnp.max(norms2.reshape(tk // 8, 8, HEAD_W), axis=0)
            k2_max = tile_max if k2_max is None else jnp.maximum(k2_max, tile_max)
            sub = lax.broadcasted_iota(jnp.int32, (V_ONES_ROWS, tk), 0)
            vt_ref[c, 0:HEAD_W, :] = v_ref[0, rows, :].astype(F32).T.astype(BF16)
            vt_ref[c, HEAD_W:HEAD_W + V_ONES_ROWS, :] = jnp.where(sub == 0, 1.0, 0.0).astype(BF16)
        k2_ref[...] = jnp.broadcast_to(jnp.max(k2_max, axis=0, keepdims=True) * NORM_SLACK, (8, HEAD_W))
        key_idx = lax.broadcasted_iota(jnp.int32, (tk, tq), 0)
        qry_idx = lax.broadcasted_iota(jnp.int32, (tk, tq), 1)
        for d in range(ndiag):
            rel = key_idx - qry_idx + d * tk
            corr_ref[d] = (-2.0 * sig) * jnp.maximum(rel, 0).astype(F32)

    lam = (jnp.exp(jnp.sum(lq1_ref[...] * lk1_ref[...], axis=1, keepdims=True))
           - jnp.exp(jnp.sum(lq2_ref[...] * lk2_ref[...], axis=1, keepdims=True)) + lam_init)
    zero_rows = jnp.zeros((HEAD_W - BIAS_ROWS, tq), BF16)
    k2 = k2_ref[0:1, :]
    k2_maps = (k2[:, 0:1], k2[:, 1:2])
    bias_row = lax.broadcasted_iota(jnp.int32, (BIAS_ROWS, tq), 0)

    def block_setup(blk):
        first = (i * nblk + blk) * ndiag
        q_f32 = q_ref[0, blk * tq:(blk + 1) * tq, :].astype(F32).T
        q_t = q_f32.astype(BF16)
        q_row = lax.broadcasted_iota(jnp.int32, (HEAD_W, tq), 0)
        q_maps = (jnp.where(q_row < QK_DIM, q_t, jnp.zeros_like(q_t)),
                  jnp.where(q_row >= QK_DIM, q_t, jnp.zeros_like(q_t)))
        qq = q_f32 * q_f32
        bounds = [jnp.sqrt(jnp.sum(qq[m * QK_DIM:(m + 1) * QK_DIM], axis=0, keepdims=True) * k2_maps[m])
                  * NORM_SLACK for m in range(2)]
        pos = lax.broadcasted_iota(jnp.int32, (BIAS_ROWS, tq), 1) + first * tk
        p_lo = pos & (POS_SPLIT - 1)
        p_hi = pos - p_lo
        alibi = jnp.where(bias_row == 0, -sig * p_lo.astype(F32),
                          jnp.where(bias_row == 1, -sig * p_hi.astype(F32),
                                    jnp.where(bias_row < 4, 1.0, 0.0)))
        chunk_ids, starts, wrapped = [], [], []
        for r in range(nchunk):
            c = first + r
            w = c >= nchunk
            c = jnp.where(w, c - nchunk, c)
            chunk_ids.append(c)
            starts.append(pl.multiple_of(c * tk, tk))
            wrapped.append(w)
        return dict(q_maps=q_maps, bounds=bounds, alibi=alibi, chunk_ids=chunk_ids, starts=starts,
                    wrapped=wrapped)

    def operand(q_m, bias_left, bias_right, r, wrapped):
        bias = bias_left if r < ndiag else jnp.where(wrapped, bias_left, bias_right)
        return jnp.concatenate([q_m, bias, zero_rows], axis=0)

    def tile_max(s_t):
        return jnp.max(s_t.reshape(tk // 8, 8, tq), axis=0)

    setups = [block_setup(blk) for blk in range(nblk)]
    largest = None
    for blk in range(nblk):
        for m in range(2):
            bound = setups[blk]["bounds"][m]
            shift_ref[blk, m] = jnp.broadcast_to(bound, (8, tq))
            largest = bound if largest is None else jnp.maximum(largest, bound)
    bound_ok = jnp.max(largest) <= ATTN_SAFE_SHIFT

    @pl.when(jnp.logical_not(bound_ok))
    def _exact_row_max():
        for blk in range(nblk):
            st = setups[blk]
            left = st["alibi"].astype(BF16)
            right = (-st["alibi"]).astype(BF16)
            for m in range(2):
                mx = None
                for r in range(ndiag):
                    s_t = jnp.dot(ka_ref[pl.ds(st["starts"][r], tk), :],
                                  operand(st["q_maps"][m], left, right, r, None),
                                  preferred_element_type=F32) + corr_ref[r]
                    mx = tile_max(s_t) if mx is None else jnp.maximum(mx, tile_max(s_t))

                def body(r, mx, blk=blk, st=st, m=m, left=left, right=right):
                    c = (i * nblk + blk) * ndiag + r
                    w = c >= nchunk
                    c = jnp.where(w, c - nchunk, c)
                    s_t = jnp.dot(ka_ref[pl.ds(pl.multiple_of(c * tk, tk), tk), :],
                                  operand(st["q_maps"][m], left, right, ndiag, w),
                                  preferred_element_type=F32)
                    return jnp.maximum(mx, tile_max(s_t))

                mx = lax.fori_loop(ndiag, nchunk, body, mx)
                shift_ref[blk, m] = jnp.broadcast_to(jnp.max(mx, axis=0, keepdims=True), (8, tq))

    def shifted_bias(blk, m):
        neg = -shift_ref[blk, m][0:1]
        hi = neg.astype(BF16).astype(F32)
        lo = (neg - hi).astype(BF16).astype(F32)
        alibi = setups[blk]["alibi"]
        with_shift = lambda a: jnp.where(bias_row == 4, hi, jnp.where(bias_row == 5, lo, a)).astype(BF16)
        return with_shift(alibi), with_shift(-alibi)

    def finish_block(blk, acc0, acc1):
        out0 = acc0[0:HEAD_W] / acc0[HEAD_W:HEAD_W + 1]
        out1 = acc1[0:HEAD_W] / acc1[HEAD_W:HEAD_W + 1]
        o = (out0 - lam * out1).T
        o = _rms(o, g_ref[...]) * (1.0 - lam_init)
        o_ref[0, blk * tq:(blk + 1) * tq, :] = o.astype(o_ref.dtype)

    stream = [(blk, r) for blk in range(nblk) for r in range(nchunk)]
    biases, probs, acc = {}, {}, {}
    for g in range(len(stream) + ATTN_VALUE_LAG):
        if g < len(stream):
            blk, r = stream[g]
            st = setups[blk]
            if r == 0:
                biases[blk] = [shifted_bias(blk, m) for m in range(2)]
            for m in range(2):
                left, right = biases[blk][m]
                s_t = jnp.dot(ka_ref[pl.ds(st["starts"][r], tk), :],
                              operand(st["q_maps"][m], left, right, r, st["wrapped"][r]),
                              preferred_element_type=F32)
                if r < ndiag:
                    s_t = s_t + corr_ref[r]
                probs[g, m] = jnp.exp(s_t).astype(BF16)
        if g >= ATTN_VALUE_LAG:
            blk, r = stream[g - ATTN_VALUE_LAG]
            v_t = vt_ref[setups[blk]["chunk_ids"][r]]
            for m in range(2):
                part = jnp.dot(v_t, probs.pop((g - ATTN_VALUE_LAG, m)), preferred_element_type=F32)
                acc[blk, m] = part if r == 0 else acc[blk, m] + part
            if r == nchunk - 1:
                finish_block(blk, acc.pop((blk, 0)), acc.pop((blk, 1)))


def _attention(q, k, v, slopes, lq1, lk1, lq2, lk2, subln_g, lam_init):
    B, S, W = q.shape
    H = W // HEAD_W
    tk = ATTN_KEY_CHUNK
    tq = ATTN_QUERY_BLOCK
    step_q = tq * ATTN_BLOCKS_PER_STEP
    kernel = functools.partial(_attn_kernel, lam_init, tq)
    vec = lambda a: a.reshape(1, -1).astype(F32)
    small = lambda n: pl.BlockSpec((1, n), lambda b, h, i: (0, 0))
    return pl.pallas_call(
        kernel,
        grid=(B, H, S // step_q),
        in_specs=[
            pl.BlockSpec((1, step_q, HEAD_W), lambda b, h, i: (b, i, h)),
            pl.BlockSpec((1, S, HEAD_W), lambda b, h, i: (b, 0, h)),
            pl.BlockSpec((1, S, HEAD_W), lambda b, h, i: (b, 0, h)),
            pl.BlockSpec((1, 1, HEAD_W), lambda b, h, i: (h, 0, 0)),
            small(QK_DIM), small(QK_DIM), small(QK_DIM), small(QK_DIM), small(HEAD_W),
        ],
        out_specs=pl.BlockSpec((1, step_q, HEAD_W), lambda b, h, i: (b, i, h)),
        out_shape=jax.ShapeDtypeStruct((B, S, W), BF16),
        scratch_shapes=[
            pltpu.VMEM((S, 2 * HEAD_W), BF16),
            pltpu.VMEM((S // tk, HEAD_W + V_ONES_ROWS, tk), BF16),
            pltpu.VMEM((tq // tk, tk, tq), F32),
            pltpu.VMEM((8, HEAD_W), F32),
            pltpu.VMEM((ATTN_BLOCKS_PER_STEP, 2, 8, tq), F32),
        ],
        compiler_params=_params("parallel", "parallel", "arbitrary"),
        name="diff_attention",
    )(q, k, v, slopes, vec(lq1), vec(lk1), vec(lq2), vec(lk2), vec(subln_g))


def _scan8(a, b, reverse):
    row = lax.broadcasted_iota(jnp.int32, a.shape, 0)
    for k in (1, 2, 4):
        if reverse:
            keep = row < 8 - k
            shift = 8 - k
        else:
            keep = row >= k
            shift = k
        a_s = jnp.where(keep, pltpu.roll(a, shift, 0), 1.0)
        b_s = jnp.where(keep, pltpu.roll(b, shift, 0), 0.0)
        b = a * b_s + b
        a = a * a_s
    return a, b


GELU_C = 0.7978845608028654
GELU_CUBIC = 0.044715


def _lru_kernel(xr_ref, yg_ref, cw_ref, cb_ref, wg_ref, lam_ref, o_ref,
                xp_ref, af_ref, uf_ref, ab_ref, ub_ref):
    S = xr_ref.shape[1]
    C = LRU_HEAD_DIM
    tc = 512
    pad = 8
    xp_ref[0:pad, :] = jnp.zeros((pad, C), F32)
    xp_ref[S + pad:S + 2 * pad, :] = jnp.zeros((pad, C), F32)
    xp_ref[pad:S + pad, :] = xr_ref[0]

    lam = lam_ref[0]
    half_cls = (0.5 * LRU_C) * (jnp.minimum(lam, 0.0) - jnp.log1p(jnp.exp(-jnp.abs(lam))))
    cw = cw_ref[...]
    lane = lax.broadcasted_iota(jnp.int32, (tc, C), 1)
    ones_lanes = jnp.where(lane < 2, 1.0, 0.0).astype(BF16)
    for c in range(S // tc):
        base = c * tc
        xc = cb_ref[...]
        for t in range(CONV_WIDTH):
            off = base + pad - CONV_WIDTH // 2 + t
            xc = xc + cw[t:t + 1, :] * xp_ref[off:off + tc, :]
        lhs = jnp.concatenate([xc.astype(BF16), ones_lanes], axis=1)
        t_gates = jnp.tanh(jnp.dot(lhs, wg_ref[0], preferred_element_type=F32))
        half_x = 0.5 * xc
        for d, (a_ref, u_ref) in enumerate(((af_ref, uf_ref), (ab_ref, ub_ref))):
            t_r = t_gates[:, (2 * d) * C:(2 * d + 1) * C]
            t_i = t_gates[:, (2 * d + 1) * C:(2 * d + 2) * C]
            hc = half_cls[:, d * C:(d + 1) * C]
            log_a = t_r * hc + hc
            a = jnp.exp(log_a)
            one_minus_a2 = jnp.tanh(log_a) * (-1.0 - a * a)
            a_ref[base:base + tc, :] = a
            root = jnp.where(one_minus_a2 > 0.0, one_minus_a2 * lax.rsqrt(one_minus_a2), 0.0)
            u_ref[base:base + tc, :] = root * (t_i * half_x + half_x)

    unroll = 8
    span = 8 * unroll

    def body(it, carry):
        cf, cb = carry
        f0 = pl.multiple_of(it * span, span)
        b0 = pl.multiple_of(S - (it + 1) * span, span)
        af = af_ref[pl.ds(f0, span), :]
        uf = uf_ref[pl.ds(f0, span), :]
        hs = []
        for j in range(unroll):
            a, b = _scan8(af[8 * j:8 * j + 8], uf[8 * j:8 * j + 8], False)
            hs.append(a * cf + b)
            cf = jnp.broadcast_to(a[7:8], (8, C)) * cf + jnp.broadcast_to(b[7:8], (8, C))
        uf_ref[pl.ds(f0, span), :] = jnp.concatenate(hs, axis=0)
        ab = ab_ref[pl.ds(b0, span), :]
        ub = ub_ref[pl.ds(b0, span), :]
        hs = []
        for j in reversed(range(unroll)):
            a, b = _scan8(ab[8 * j:8 * j + 8], ub[8 * j:8 * j + 8], True)
            hs.append(a * cb + b)
            cb = jnp.broadcast_to(a[0:1], (8, C)) * cb + jnp.broadcast_to(b[0:1], (8, C))
        ub_ref[pl.ds(b0, span), :] = jnp.concatenate(hs[::-1], axis=0)
        return cf, cb

    zero = jnp.zeros((8, C), F32)
    lax.fori_loop(0, S // span, body, (zero, zero))

    for c in range(S // tc):
        rows = slice(c * tc, (c + 1) * tc)
        y = yg_ref[0, rows, :]
        half_y = 0.5 * y
        gelu = half_y * jnp.tanh(y * ((y * y) * (GELU_C * GELU_CUBIC) + GELU_C)) + half_y
        o_ref[0, rows, :] = ((uf_ref[rows, :] + ub_ref[rows, :]) * gelu).astype(o_ref.dtype)


def _lru(xr, yg, conv_w, conv_b, w_a, b_a, w_i, b_i, lam):
    B, S, W = xr.shape
    C = LRU_HEAD_DIM
    H = W // C
    per_head = lambda a: a.reshape(H, 1, C)
    wg = 0.5 * jnp.concatenate([w_a[0], w_i[0], w_a[1], w_i[1]], axis=-1)
    bg = 0.5 * jnp.concatenate([per_head(b_a[0]), per_head(b_i[0]), per_head(b_a[1]), per_head(b_i[1])], axis=-1)
    bg_hi = bg.astype(BF16)
    bg_lo = (bg - bg_hi.astype(F32)).astype(BF16)
    wg = jnp.concatenate([wg.astype(BF16), bg_hi, bg_lo, jnp.zeros((H, C - 2, 4 * C), BF16)], axis=1)
    lam2 = jnp.concatenate([per_head(lam[0]), per_head(lam[1])], axis=-1)
    seq = pl.BlockSpec((1, S, C), lambda b, h: (b, 0, h))
    return pl.pallas_call(
        _lru_kernel,
        grid=(B, H),
        in_specs=[
            seq, seq,
            pl.BlockSpec((CONV_WIDTH, C), lambda b, h: (0, h)),
            pl.BlockSpec((1, C), lambda b, h: (0, h)),
            pl.BlockSpec((1, 2 * C, 4 * C), lambda b, h: (h, 0, 0)),
            pl.BlockSpec((1, 1, 2 * C), lambda b, h: (h, 0, 0)),
        ],
        out_specs=seq,
        out_shape=jax.ShapeDtypeStruct((B, S, W), BF16),
        scratch_shapes=[pltpu.VMEM((S + 16, C), F32)] + [pltpu.VMEM((S, C), F32) for _ in range(4)],
        compiler_params=_params("parallel", "parallel"),
        name="rg_lru",
    )(xr, yg, conv_w, conv_b.reshape(1, W), wg, lam2)


@functools.lru_cache(maxsize=None)
def _dft_tables(seq, group_dim):
    n4 = seq // FOURIER_RADIX
    k1 = np.arange(n4, dtype=np.int64)[:, None]
    s1 = np.arange(n4, dtype=np.int64)[None, :]
    cos_t, sin_t = [], []
    for s2 in range(FOURIER_RADIX):
        ang = 2.0 * np.pi * ((k1 * (FOURIER_RADIX * s1 + s2)) % seq).astype(np.float64) / seq
        cos_t.append(np.cos(ang) / np.sqrt(seq))
        sin_t.append(-np.sin(ang) / np.sqrt(seq))
    c = np.arange(group_dim, dtype=np.int64)
    ang = 2.0 * np.pi * ((c[:, None] * c[None, :]) % group_dim).astype(np.float64) / group_dim
    chan = np.concatenate([np.cos(ang), np.sin(ang)], axis=0) / np.sqrt(group_dim)
    return (np.stack(cos_t).astype(np.float32), np.stack(sin_t).astype(np.float32), chan.astype(np.float32))


def _fourier_kernel(h0_ref, h1_ref, h2_ref, h3_ref, fc_ref, fs_ref, cs_ref, o_ref):
    n4 = fc_ref.shape[1]
    rows = 256
    nkc = n4 // rows

    def sequence_dft(kc):
        rs = slice(kc * rows, (kc + 1) * rows)
        yr, yi = [], []
        for r, h_ref in enumerate((h0_ref, h1_ref, h2_ref, h3_ref)):
            x = h_ref[0, 0]
            yr.append(jnp.dot(fc_ref[r, rs, :], x, preferred_element_type=F32))
            yi.append(jnp.dot(fs_ref[r, rs, :], x, preferred_element_type=F32))
        xs = (
            (yr[0] + yr[1] + yr[2] + yr[3], yi[0] + yi[1] + yi[2] + yi[3]),
            (yr[0] + yi[1] - yr[2] - yi[3], yi[0] - yr[1] - yi[2] + yr[3]),
            (yr[0] - yr[1] + yr[2] - yr[3], yi[0] - yi[1] + yi[2] - yi[3]),
            (yr[0] - yi[1] - yr[2] + yi[3], yi[0] + yr[1] - yi[2] - yr[3]),
        )
        return [jnp.concatenate([xr.astype(BF16), xi.astype(BF16)], axis=1) for xr, xi in xs]

    def channel_dft(kc, zs):
        for k2, z in enumerate(zs):
            f = jnp.dot(z, cs_ref[...], preferred_element_type=F32)
            o_ref[0, k2 * n4 + kc * rows:k2 * n4 + (kc + 1) * rows, :] = f.astype(o_ref.dtype)

    pending = None
    for kc in range(nkc):
        zs = sequence_dft(kc)
        if pending is not None:
            channel_dft(kc - 1, pending)
        pending = zs
    channel_dft(nkc - 1, pending)


def _fourier(h4):
    B, _, n4, D = h4.shape
    S = n4 * FOURIER_RADIX
    G = D // FOURIER_GROUPS
    fc, fs, cs = (jnp.asarray(t).astype(BF16) for t in _dft_tables(S, G))
    per_res = lambda r: pl.BlockSpec((1, 1, n4, G), lambda b, g: (b, r, 0, g))
    return pl.pallas_call(
        _fourier_kernel,
        grid=(B, FOURIER_GROUPS),
        in_specs=[per_res(0), per_res(1), per_res(2), per_res(3),
                  _resident(fc.shape), _resident(fs.shape), _resident(cs.shape)],
        out_specs=pl.BlockSpec((1, S, G), lambda b, g: (b, 0, g)),
        out_shape=jax.ShapeDtypeStruct((B, S, D), BF16),
        compiler_params=_params("parallel", "parallel"),
        name="fourier",
    )(h4, h4, h4, h4, fc, fs, cs)


FFN_ROW_SPLITS = 2
FFN_COL_CHUNK = 256


def _post_ffn_kernel(emit_next, x_ref, a_ref, b_ref, wa_ref, wb_ref, gpost_ref, gpre_ref, wg_ref, wu_ref,
                     wd_ref, gfpost_ref, gnext_ref, pick_ref, o_ref, *rest):
    act_ref = rest[-1]
    tm = x_ref.shape[0]
    dff = wg_ref.shape[1]
    groups = [slice(s * (tm // FFN_ROW_SPLITS), (s + 1) * (tm // FFN_ROW_SPLITS)) for s in range(FFN_ROW_SPLITS)]

    def mix(rows):
        return (jnp.dot(a_ref[rows, :], wa_ref[...], preferred_element_type=F32)
                + jnp.dot(b_ref[rows, :], wb_ref[...], preferred_element_type=F32))

    def gate_up(rows, h):
        for c in range(dff // FFN_COL_CHUNK):
            cols = slice(c * FFN_COL_CHUNK, (c + 1) * FFN_COL_CHUNK)
            g = jnp.dot(h, wg_ref[:, cols], preferred_element_type=F32)
            u = jnp.dot(h, wu_ref[:, cols], preferred_element_type=F32)
            half_g = 0.5 * g
            act_ref[rows, cols] = ((half_g * jnp.tanh(half_g) + half_g) * u).astype(BF16)

    mixed = [mix(rows) for rows in groups]
    xs = []
    for rows, m in zip(groups, mixed):
        x = x_ref[rows, :] + _rms(m, gpost_ref[...])
        xs.append(x)
        gate_up(rows, _rms(x, gpre_ref[...]).astype(BF16))
    normed = []
    for rows, x in zip(groups, xs):
        y = jnp.dot(act_ref[rows, :], wd_ref[...], preferred_element_type=F32)
        x = x + _rms(y, gfpost_ref[...])
        o_ref[rows, :] = x
        if emit_next:
            normed.append(_rms(x, gnext_ref[...]).astype(BF16))
    for s, hn in enumerate(normed):
        hn_ref = rest[0]
        n = hn.shape[0] // FOURIER_RADIX
        for r in range(FOURIER_RADIX):
            picked = jnp.dot(pick_ref[r], hn, preferred_element_type=F32)
            hn_ref[0, r, s * n:(s + 1) * n, :] = picked.astype(hn_ref.dtype)


def _post_ffn(x, a, a_col, b, b_col, w_mix, g_post, g_pre, wg, wu, wd, g_fpost, g_next, seq, tm=512):
    T, D = x.shape
    half = w_mix.shape[0] // 2
    dff = wg.shape[1]
    emit_next = g_next is not None
    rows_per_group = tm // FFN_ROW_SPLITS
    n_pick = rows_per_group // FOURIER_RADIX
    pick = np.zeros((FOURIER_RADIX, n_pick, rows_per_group), np.float32)
    for r in range(FOURIER_RADIX):
        pick[r, np.arange(n_pick), FOURIER_RADIX * np.arange(n_pick) + r] = 1.0
    pick = jnp.asarray(pick, BF16)
    row = lambda i: (i, 0)
    vec = lambda g: g.reshape(1, D)
    out_specs = [pl.BlockSpec((tm, D), row)]
    out_shape = [jax.ShapeDtypeStruct((T, D), F32)]
    if emit_next:
        steps = seq // tm
        out_specs.append(pl.BlockSpec((1, FOURIER_RADIX, tm // FOURIER_RADIX, D),
                                      lambda i: (i // steps, 0, i % steps, 0)))
        out_shape.append(jax.ShapeDtypeStruct((T // seq, FOURIER_RADIX, seq // FOURIER_RADIX, D), BF16))
    outs = pl.pallas_call(
        functools.partial(_post_ffn_kernel, emit_next),
        grid=(T // tm,),
        in_specs=[
            pl.BlockSpec((tm, D), row),
            pl.BlockSpec((tm, half), lambda i: (i, a_col)),
            pl.BlockSpec((tm, half), lambda i: (i, b_col)),
            pl.BlockSpec((half, D), lambda i: (0, 0), pipeline_mode=pl.Buffered(1)),
            pl.BlockSpec((half, D), lambda i: (1, 0), pipeline_mode=pl.Buffered(1)),
            _resident((1, D)), _resident((1, D)),
            _resident(wg.shape), _resident(wu.shape), _resident(wd.shape),
            _resident((1, D)), _resident((1, D)), _resident(pick.shape),
        ],
        out_specs=out_specs,
        out_shape=out_shape,
        scratch_shapes=[pltpu.VMEM((tm, dff), BF16)],
        compiler_params=_params("parallel"),
        name="post_ffn",
    )(x, a, b, w_mix, w_mix, vec(g_post), vec(g_pre), wg, wu, wd, vec(g_fpost),
      vec(g_next if emit_next else g_fpost), pick)
    return (outs[0], outs[1]) if emit_next else (outs[0], None)


def kernel(x, ln_mix_pre, ln_mix_post, ln_ffn_pre, ln_ffn_post, w_in, w_mix_out, lambda_q1, lambda_k1,
           lambda_q2, lambda_k2, attn_subln, conv_w, conv_b, lru_w_a, lru_b_a, lru_w_i, lru_b_i, lru_lambda,
           w_fourier_out, w_ffn_gate, w_ffn_up, w_ffn_down):
    B, S, D = x.shape
    T = B * S
    depth = ln_mix_pre.shape[0]
    heads = jnp.arange(1, ATTN_HEADS + 1, dtype=F32)
    slopes = jnp.broadcast_to(jnp.exp2(-8.0 * heads / ATTN_HEADS)[:, None, None], (ATTN_HEADS, 1, HEAD_W))
    xf = x.reshape(T, D)
    h4 = None
    for l in range(depth):
        if l % 2 == 0:
            e = l // 2
            lam_init = 0.8 - 0.6 * float(np.exp(-0.3 * l))
            q, k, v, xr, yg = _inproj(xf, ln_mix_pre[l], w_in[e].astype(BF16))
            width = q.shape[1]
            to_seq = lambda a: a.reshape(B, S, width)
            attn = _attention(to_seq(q), to_seq(k), to_seq(v), slopes, lambda_q1[e], lambda_k1[e],
                              lambda_q2[e], lambda_k2[e], attn_subln[e], lam_init)
            rec = _lru(to_seq(xr), to_seq(yg), conv_w[e], conv_b[e], lru_w_a[e], lru_b_a[e],
                       lru_w_i[e], lru_b_i[e], lru_lambda[e])
            a, a_col = attn.reshape(T, width), 0
            b, b_col = rec.reshape(T, width), 0
            w_mix = w_mix_out[e]
        else:
            f = _fourier(h4).reshape(T, D)
            a, a_col, b, b_col = f, 0, f, 1
            w_mix = w_fourier_out[l // 2]
        next_is_odd = l + 1 < depth and (l + 1) % 2 == 1
        xf, h4 = _post_ffn(xf, a, a_col, b, b_col, w_mix.astype(BF16), ln_mix_post[l], ln_ffn_pre[l],
                           w_ffn_gate[l].astype(BF16), w_ffn_up[l].astype(BF16), w_ffn_down[l].astype(BF16),
                           ln_ffn_post[l], ln_mix_pre[l + 1] if next_is_odd else None, S)
    return xf.reshape(B, S, D)
```

```python
import functools

import numpy as np
import jax
import jax.numpy as jnp
from jax import lax
from jax.experimental import pallas as pl
from jax.experimental.pallas import tpu as pltpu

F32 = jnp.float32
BF16 = jnp.bfloat16

EPS = 1e-6
ATTN_HEADS = 4
QK_DIM = 64
HEAD_W = 2 * QK_DIM
LRU_HEADS = 4
LRU_HEAD_DIM = 128
CONV_WIDTH = 4
LRU_C = 8.0
FOURIER_GROUPS = 4
FOURIER_RADIX = 4

VMEM_LIMIT_BYTES = 54 * 1024 * 1024
POS_SPLIT = 64


def _rms(x, g):
    ms = jnp.mean(x * x, axis=-1, keepdims=True)
    return x * lax.rsqrt(ms + EPS) * g


def _params(*semantics, flags=None):
    return pltpu.CompilerParams(dimension_semantics=semantics, vmem_limit_bytes=VMEM_LIMIT_BYTES, flags=flags)


def _resident(shape):
    nd = len(shape)
    return pl.BlockSpec(shape, lambda *_: (0,) * nd, pipeline_mode=pl.Buffered(1))


def _inproj_kernel(x_ref, g_ref, w_ref, q_ref, k_ref, v_ref, xr_ref, yg_ref):
    h = _rms(x_ref[...], g_ref[...]).astype(BF16)
    width = q_ref.shape[1]
    for n, o_ref in enumerate((q_ref, k_ref, v_ref, xr_ref, yg_ref)):
        z = jnp.dot(h, w_ref[:, n * width:(n + 1) * width], preferred_element_type=F32)
        if n == 0:
            z = z * (QK_DIM ** -0.5)
        o_ref[...] = z.astype(o_ref.dtype)


def _inproj(x, g, w, tm=512):
    T, D = x.shape
    width = w.shape[1] // 5
    row = lambda i: (i, 0)
    out_shapes = [jax.ShapeDtypeStruct((T, width), dt) for dt in (BF16, BF16, BF16, F32, F32)]
    return pl.pallas_call(
        _inproj_kernel,
        grid=(T // tm,),
        in_specs=[pl.BlockSpec((tm, D), row), _resident((1, D)), _resident(w.shape)],
        out_specs=[pl.BlockSpec((tm, width), row) for _ in out_shapes],
        out_shape=out_shapes,
        compiler_params=_params("parallel"),
        name="inproj",
    )(x, g.reshape(1, D), w)


ATTN_KEY_CHUNK = 256
ATTN_QUERY_BLOCK = 512
ATTN_BLOCKS_PER_STEP = 4
ATTN_VALUE_LAG = 1
BIAS_ROWS = 16
ATTN_SAFE_SHIFT = 30.0
NORM_SLACK = 1.0 + 2.0 ** -5


def _attn_kernel(lam_init, tq, q_ref, k_ref, v_ref, slope_ref, lq1_ref, lk1_ref, lq2_ref, lk2_ref,
                 g_ref, o_ref, ka_ref, vt_ref, corr_ref, k2_ref, shift_ref):
    S = k_ref.shape[1]
    tk = ATTN_KEY_CHUNK
    nchunk = S // tk
    ndiag = tq // tk
    nblk = ATTN_BLOCKS_PER_STEP
    i = pl.program_id(2)
    sig_row = slope_ref[0]
    sig = sig_row[:, 0:1]

    @pl.when(i == 0)
    def _build_keys_values():
        row = lax.broadcasted_iota(jnp.int32, (HEAD_W, HEAD_W), 0)
        col = lax.broadcasted_iota(jnp.int32, (HEAD_W, HEAD_W), 1)
        per_map = jnp.where(col == 0, jnp.where(row < QK_DIM, 1.0, 0.0),
                            jnp.where(col == 1, jnp.where(row >= QK_DIM, 1.0, 0.0), 0.0)).astype(BF16)
        k2_max = None
        for c in range(nchunk):
            rows = slice(c * tk, (c + 1) * tk)
            kb = k_ref[0, rows, :]
            lane = lax.broadcasted_iota(jnp.int32, (tk, HEAD_W), 1)
            j = lax.broadcasted_iota(jnp.int32, (tk, HEAD_W), 0) + c * tk
            j_lo = j & (POS_SPLIT - 1)
            j_hi = j - j_lo
            aug = jnp.where(lane < 2, 1.0,
                            jnp.where(lane == 2, sig_row * j_lo.astype(F32),
                                      jnp.where(lane == 3, sig_row * j_hi.astype(F32),
                                                jnp.where(lane < 6, 1.0, 0.0))))
            ka_ref[rows, 0:HEAD_W] = kb
            ka_ref[rows, HEAD_W:2 * HEAD_W] = aug.astype(BF16)
            kf = kb.astype(F32)
            norms2 = jnp.dot((kf * kf).astype(BF16), per_map, preferred_element_type=F32)
            tile_max = jnp.max(norms2.reshape(tk // 8, 8, HEAD_W), axis=0)
            k2_max = tile_max if k2_max is None else jnp.maximum(k2_max, tile_max)
            vt_ref[c] = v_ref[0, rows, :].astype(F32).T.astype(BF16)
        k2_ref[...] = jnp.broadcast_to(jnp.max(k2_max, axis=0, keepdims=True) * NORM_SLACK, (8, HEAD_W))
        key_idx = lax.broadcasted_iota(jnp.int32, (tk, tq), 0)
        qry_idx = lax.broadcasted_iota(jnp.int32, (tk, tq), 1)
        for d in range(ndiag):
            rel = key_idx - qry_idx + d * tk
            corr_ref[d] = (-2.0 * sig) * jnp.maximum(rel, 0).astype(F32)

    lam = (jnp.exp(jnp.sum(lq1_ref[...] * lk1_ref[...], axis=1, keepdims=True))
           - jnp.exp(jnp.sum(lq2_ref[...] * lk2_ref[...], axis=1, keepdims=True)) + lam_init)
    zero_rows = jnp.zeros((HEAD_W - BIAS_ROWS, tq), BF16)
    k2 = k2_ref[0:1, :]
    k2_maps = (k2[:, 0:1], k2[:, 1:2])
    bias_row = lax.broadcasted_iota(jnp.int32, (BIAS_ROWS, tq), 0)

    def block_setup(blk):
        first = (i * nblk + blk) * ndiag
        q_f32 = q_ref[0, blk * tq:(blk + 1) * tq, :].astype(F32).T
        q_t = q_f32.astype(BF16)
        q_row = lax.broadcasted_iota(jnp.int32, (HEAD_W, tq), 0)
        q_maps = (jnp.where(q_row < QK_DIM, q_t, jnp.zeros_like(q_t)),
                  jnp.where(q_row >= QK_DIM, q_t, jnp.zeros_like(q_t)))
        qq = q_f32 * q_f32
        bounds = [jnp.sqrt(jnp.sum(qq[m * QK_DIM:(m + 1) * QK_DIM], axis=0, keepdims=True) * k2_maps[m])
                  * NORM_SLACK for m in range(2)]
        pos = lax.broadcasted_iota(jnp.int32, (BIAS_ROWS, tq), 1) + first * tk
        p_lo = pos & (POS_SPLIT - 1)
        p_hi = pos - p_lo
        alibi = jnp.where(bias_row == 0, -sig * p_lo.astype(F32),
                          jnp.where(bias_row == 1, -sig * p_hi.astype(F32),
                                    jnp.where(bias_row < 4, 1.0, 0.0)))
        chunk_ids, starts, wrapped = [], [], []
        for r in range(nchunk):
            c = first + r
            w = c >= nchunk
            c = jnp.where(w, c - nchunk, c)
            chunk_ids.append(c)
            starts.append(pl.multiple_of(c * tk, tk))
            wrapped.append(w)
        return dict(q_maps=q_maps, bounds=bounds, alibi=alibi, chunk_ids=chunk_ids, starts=starts,
                    wrapped=wrapped)

    def operand(q_m, bias_left, bias_right, r, wrapped):
        bias = bias_left if r < ndiag else jnp.where(wrapped, bias_left, bias_right)
        return jnp.concatenate([q_m, bias, zero_rows], axis=0)

    def tile_max(s_t):
        return jnp.max(s_t.reshape(tk // 8, 8, tq), axis=0)

    setups = [block_setup(blk) for blk in range(nblk)]
    largest = None
    for blk in range(nblk):
        for m in range(2):
            bound = setups[blk]["bounds"][m]
            shift_ref[blk, m] = jnp.broadcast_to(bound, (8, tq))
            largest = bound if largest is None else jnp.maximum(largest, bound)
    bound_ok = jnp.max(largest) <= ATTN_SAFE_SHIFT

    @pl.when(jnp.logical_not(bound_ok))
    def _exact_row_max():
        for blk in range(nblk):
            st = setups[blk]
            left = st["alibi"].astype(BF16)
            right = (-st["alibi"]).astype(BF16)
            for m in range(2):
                mx = None
                for r in range(ndiag):
                    s_t = jnp.dot(ka_ref[pl.ds(st["starts"][r], tk), :],
                                  operand(st["q_maps"][m], left, right, r, None),
                                  preferred_element_type=F32) + corr_ref[r]
                    mx = tile_max(s_t) if mx is None else jnp.maximum(mx, tile_max(s_t))

                def body(r, mx, blk=blk, st=st, m=m, left=left, right=right):
                    c = (i * nblk + blk) * ndiag + r
                    w = c >= nchunk
                    c = jnp.where(w, c - nchunk, c)
                    s_t = jnp.dot(ka_ref[pl.ds(pl.multiple_of(c * tk, tk), tk), :],
                                  operand(st["q_maps"][m], left, right, ndiag, w),
                                  preferred_element_type=F32)
                    return jnp.maximum(mx, tile_max(s_t))

                mx = lax.fori_loop(ndiag, nchunk, body, mx)
                shift_ref[blk, m] = jnp.broadcast_to(jnp.max(mx, axis=0, keepdims=True), (8, tq))

    def shifted_bias(blk, m):
        neg = -shift_ref[blk, m][0:1]
        hi = neg.astype(BF16).astype(F32)
        lo = (neg - hi).astype(BF16).astype(F32)
        alibi = setups[blk]["alibi"]
        with_shift = lambda a: jnp.where(bias_row == 4, hi, jnp.where(bias_row == 5, lo, a)).astype(BF16)
        return with_shift(alibi), with_shift(-alibi)

    def finish_block(blk, acc0, acc1, sum0, sum1):
        out0 = acc0 / jnp.sum(sum0, axis=0, keepdims=True)
        out1 = acc1 / jnp.sum(sum1, axis=0, keepdims=True)
        o = (out0 - lam * out1).T
        o = _rms(o, g_ref[...]) * (1.0 - lam_init)
        o_ref[0, blk * tq:(blk + 1) * tq, :] = o.astype(o_ref.dtype)

    stream = [(blk, r) for blk in range(nblk) for r in range(nchunk)]
    biases, probs, acc, sums = {}, {}, {}, {}
    for g in range(len(stream) + ATTN_VALUE_LAG):
        cur = stream[g] if g < len(stream) else None
        old = stream[g - ATTN_VALUE_LAG] if g >= ATTN_VALUE_LAG else None
        if cur is not None and cur[1] == 0:
            biases[cur[0]] = [shifted_bias(cur[0], m) for m in range(2)]
        for m in range(2):
            if cur is not None:
                blk, r = cur
                st = setups[blk]
                left, right = biases[blk][m]
                s_t = jnp.dot(ka_ref[pl.ds(st["starts"][r], tk), :],
                              operand(st["q_maps"][m], left, right, r, st["wrapped"][r]),
                              preferred_element_type=F32)
                if r < ndiag:
                    s_t = s_t + corr_ref[r]
                e_t = jnp.exp(s_t)
                probs[g, m] = e_t.astype(BF16)
                part_sum = jnp.sum(e_t.reshape(tk // 8, 8, tq), axis=0)
                sums[blk, m] = part_sum if r == 0 else sums[blk, m] + part_sum
            if old is not None:
                blk, r = old
                part = jnp.dot(vt_ref[setups[blk]["chunk_ids"][r]], probs.pop((g - ATTN_VALUE_LAG, m)),
                               preferred_element_type=F32)
                acc[blk, m] = part if r == 0 else acc[blk, m] + part
        if old is not None and old[1] == nchunk - 1:
            finish_block(old[0], acc.pop((old[0], 0)), acc.pop((old[0], 1)),
                         sums.pop((old[0], 0)), sums.pop((old[0], 1)))


def _attention(q, k, v, slopes, lq1, lk1, lq2, lk2, subln_g, lam_init):
    B, S, W = q.shape
    H = W // HEAD_W
    tk = ATTN_KEY_CHUNK
    tq = ATTN_QUERY_BLOCK
    step_q = tq * ATTN_BLOCKS_PER_STEP
    kernel = functools.partial(_attn_kernel, lam_init, tq)
    vec = lambda a: a.reshape(1, -1).astype(F32)
    small = lambda n: pl.BlockSpec((1, n), lambda b, h, i: (0, 0))
    return pl.pallas_call(
        kernel,
        grid=(B, H, S // step_q),
        in_specs=[
            pl.BlockSpec((1, step_q, HEAD_W), lambda b, h, i: (b, i, h)),
            pl.BlockSpec((1, S, HEAD_W), lambda b, h, i: (b, 0, h)),
            pl.BlockSpec((1, S, HEAD_W), lambda b, h, i: (b, 0, h)),
            pl.BlockSpec((1, 1, HEAD_W), lambda b, h, i: (h, 0, 0)),
            small(QK_DIM), small(QK_DIM), small(QK_DIM), small(QK_DIM), small(HEAD_W),
        ],
        out_specs=pl.BlockSpec((1, step_q, HEAD_W), lambda b, h, i: (b, i, h)),
        out_shape=jax.ShapeDtypeStruct((B, S, W), BF16),
        scratch_shapes=[
            pltpu.VMEM((S, 2 * HEAD_W), BF16),
            pltpu.VMEM((S // tk, HEAD_W, tk), BF16),
            pltpu.VMEM((tq // tk, tk, tq), F32),
            pltpu.VMEM((8, HEAD_W), F32),
            pltpu.VMEM((ATTN_BLOCKS_PER_STEP, 2, 8, tq), F32),
        ],
        compiler_params=_params("parallel", "parallel", "arbitrary"),
        name="diff_attention",
    )(q, k, v, slopes, vec(lq1), vec(lk1), vec(lq2), vec(lk2), vec(subln_g))


def _scan8(a, b, reverse):
    row = lax.broadcasted_iota(jnp.int32, a.shape, 0)
    for k in (1, 2, 4):
        if reverse:
            keep = row < 8 - k
            shift = 8 - k
        else:
            keep = row >= k
            shift = k
        a_s = jnp.where(keep, pltpu.roll(a, shift, 0), 1.0)
        b_s = jnp.where(keep, pltpu.roll(b, shift, 0), 0.0)
        b = a * b_s + b
        a = a * a_s
    return a, b


GELU_C = 0.7978845608028654
GELU_CUBIC = 0.044715


def _lru_kernel(xr_ref, yg_ref, cw_ref, cb_ref, wg_ref, lam_ref, o_ref,
                xp_ref, af_ref, uf_ref, ab_ref, ub_ref):
    S = xr_ref.shape[1]
    C = LRU_HEAD_DIM
    tc = 512
    pad = 8
    xp_ref[0:pad, :] = jnp.zeros((pad, C), F32)
    xp_ref[S + pad:S + 2 * pad, :] = jnp.zeros((pad, C), F32)
    xp_ref[pad:S + pad, :] = xr_ref[0]

    lam = lam_ref[0]
    half_cls = (0.5 * LRU_C) * (jnp.minimum(lam, 0.0) - jnp.log1p(jnp.exp(-jnp.abs(lam))))
    cw = cw_ref[...]
    lane = lax.broadcasted_iota(jnp.int32, (tc, C), 1)
    ones_lanes = jnp.where(lane < 2, 1.0, 0.0).astype(BF16)
    for c in range(S // tc):
        base = c * tc
        xc = cb_ref[...]
        for t in range(CONV_WIDTH):
            off = base + pad - CONV_WIDTH // 2 + t
            xc = xc + cw[t:t + 1, :] * xp_ref[off:off + tc, :]
        lhs = jnp.concatenate([xc.astype(BF16), ones_lanes], axis=1)
        t_gates = jnp.tanh(jnp.dot(lhs, wg_ref[0], preferred_element_type=F32))
        half_x = 0.5 * xc
        for d, (a_ref, u_ref) in enumerate(((af_ref, uf_ref), (ab_ref, ub_ref))):
            t_r = t_gates[:, (2 * d) * C:(2 * d + 1) * C]
            t_i = t_gates[:, (2 * d + 1) * C:(2 * d + 2) * C]
            hc = half_cls[:, d * C:(d + 1) * C]
            log_a = t_r * hc + hc
            a = jnp.exp(log_a)
            one_minus_a2 = jnp.tanh(log_a) * (-1.0 - a * a)
            a_ref[base:base + tc, :] = a
            root = jnp.where(one_minus_a2 > 0.0, one_minus_a2 * lax.rsqrt(one_minus_a2), 0.0)
            u_ref[base:base + tc, :] = root * (t_i * half_x + half_x)

    unroll = 8
    span = 8 * unroll

    def body(it, carry):
        cf, cb = carry
        f0 = pl.multiple_of(it * span, span)
        b0 = pl.multiple_of(S - (it + 1) * span, span)
        af = af_ref[pl.ds(f0, span), :]
        uf = uf_ref[pl.ds(f0, span), :]
        hs = []
        for j in range(unroll):
            a, b = _scan8(af[8 * j:8 * j + 8], uf[8 * j:8 * j + 8], False)
            hs.append(a * cf + b)
            cf = jnp.broadcast_to(a[7:8], (8, C)) * cf + jnp.broadcast_to(b[7:8], (8, C))
        uf_ref[pl.ds(f0, span), :] = jnp.concatenate(hs, axis=0)
        ab = ab_ref[pl.ds(b0, span), :]
        ub = ub_ref[pl.ds(b0, span), :]
        hs = []
        for j in reversed(range(unroll)):
            a, b = _scan8(ab[8 * j:8 * j + 8], ub[8 * j:8 * j + 8], True)
            hs.append(a * cb + b)
            cb = jnp.broadcast_to(a[0:1], (8, C)) * cb + jnp.broadcast_to(b[0:1], (8, C))
        ub_ref[pl.ds(b0, span), :] = jnp.concatenate(hs[::-1], axis=0)
        return cf, cb

    zero = jnp.zeros((8, C), F32)
    lax.fori_loop(0, S // span, body, (zero, zero))

    for c in range(S // tc):
        rows = slice(c * tc, (c + 1) * tc)
        y = yg_ref[0, rows, :]
        half_y = 0.5 * y
        gelu = half_y * jnp.tanh(y * ((y * y) * (GELU_C * GELU_CUBIC) + GELU_C)) + half_y
        o_ref[0, rows, :] = ((uf_ref[rows, :] + ub_ref[rows, :]) * gelu).astype(o_ref.dtype)


def _lru(xr, yg, conv_w, conv_b, w_a, b_a, w_i, b_i, lam):
    B, S, W = xr.shape
    C = LRU_HEAD_DIM
    H = W // C
    per_head = lambda a: a.reshape(H, 1, C)
    wg = 0.5 * jnp.concatenate([w_a[0], w_i[0], w_a[1], w_i[1]], axis=-1)
    bg = 0.5 * jnp.concatenate([per_head(b_a[0]), per_head(b_i[0]), per_head(b_a[1]), per_head(b_i[1])], axis=-1)
    bg_hi = bg.astype(BF16)
    bg_lo = (bg - bg_hi.astype(F32)).astype(BF16)
    wg = jnp.concatenate([wg.astype(BF16), bg_hi, bg_lo, jnp.zeros((H, C - 2, 4 * C), BF16)], axis=1)
    lam2 = jnp.concatenate([per_head(lam[0]), per_head(lam[1])], axis=-1)
    seq = pl.BlockSpec((1, S, C), lambda b, h: (b, 0, h))
    return pl.pallas_call(
        _lru_kernel,
        grid=(B, H),
        in_specs=[
            seq, seq,
            pl.BlockSpec((CONV_WIDTH, C), lambda b, h: (0, h)),
            pl.BlockSpec((1, C), lambda b, h: (0, h)),
            pl.BlockSpec((1, 2 * C, 4 * C), lambda b, h: (h, 0, 0)),
            pl.BlockSpec((1, 1, 2 * C), lambda b, h: (h, 0, 0)),
        ],
        out_specs=seq,
        out_shape=jax.ShapeDtypeStruct((B, S, W), BF16),
        scratch_shapes=[pltpu.VMEM((S + 16, C), F32)] + [pltpu.VMEM((S, C), F32) for _ in range(4)],
        compiler_params=_params("parallel", "parallel"),
        name="rg_lru",
    )(xr, yg, conv_w, conv_b.reshape(1, W), wg, lam2)


@functools.lru_cache(maxsize=None)
def _dft_tables(seq, group_dim):
    n4 = seq // FOURIER_RADIX
    k1 = np.arange(n4, dtype=np.int64)[:, None]
    s1 = np.arange(n4, dtype=np.int64)[None, :]
    cos_t, sin_t = [], []
    for s2 in range(FOURIER_RADIX):
        ang = 2.0 * np.pi * ((k1 * (FOURIER_RADIX * s1 + s2)) % seq).astype(np.float64) / seq
        cos_t.append(np.cos(ang) / np.sqrt(seq))
        sin_t.append(-np.sin(ang) / np.sqrt(seq))
    c = np.arange(group_dim, dtype=np.int64)
    ang = 2.0 * np.pi * ((c[:, None] * c[None, :]) % group_dim).astype(np.float64) / group_dim
    chan = np.concatenate([np.cos(ang), np.sin(ang)], axis=0) / np.sqrt(group_dim)
    return (np.stack(cos_t).astype(np.float32), np.stack(sin_t).astype(np.float32), chan.astype(np.float32))


def _fourier_kernel(h0_ref, h1_ref, h2_ref, h3_ref, fc_ref, fs_ref, cs_ref, o_ref):
    n4 = fc_ref.shape[1]
    rows = 256
    nkc = n4 // rows

    def sequence_dft(kc):
        rs = slice(kc * rows, (kc + 1) * rows)
        yr, yi = [], []
        for r, h_ref in enumerate((h0_ref, h1_ref, h2_ref, h3_ref)):
            x = h_ref[0, 0]
            yr.append(jnp.dot(fc_ref[r, rs, :], x, preferred_element_type=F32))
            yi.append(jnp.dot(fs_ref[r, rs, :], x, preferred_element_type=F32))
        xs = (
            (yr[0] + yr[1] + yr[2] + yr[3], yi[0] + yi[1] + yi[2] + yi[3]),
            (yr[0] + yi[1] - yr[2] - yi[3], yi[0] - yr[1] - yi[2] + yr[3]),
            (yr[0] - yr[1] + yr[2] - yr[3], yi[0] - yi[1] + yi[2] - yi[3]),
            (yr[0] - yi[1] - yr[2] + yi[3], yi[0] + yr[1] - yi[2] - yr[3]),
        )
        return [jnp.concatenate([xr.astype(BF16), xi.astype(BF16)], axis=1) for xr, xi in xs]

    def channel_dft(kc, zs):
        for k2, z in enumerate(zs):
            f = jnp.dot(z, cs_ref[...], preferred_element_type=F32)
            o_ref[0, k2 * n4 + kc * rows:k2 * n4 + (kc + 1) * rows, :] = f.astype(o_ref.dtype)

    pending = None
    for kc in range(nkc):
        zs = sequence_dft(kc)
        if pending is not None:
            channel_dft(kc - 1, pending)
        pending = zs
    channel_dft(nkc - 1, pending)


def _fourier(h4):
    B, _, n4, D = h4.shape
    S = n4 * FOURIER_RADIX
    G = D // FOURIER_GROUPS
    fc, fs, cs = (jnp.asarray(t).astype(BF16) for t in _dft_tables(S, G))
    per_res = lambda r: pl.BlockSpec((1, 1, n4, G), lambda b, g: (b, r, 0, g))
    return pl.pallas_call(
        _fourier_kernel,
        grid=(B, FOURIER_GROUPS),
        in_specs=[per_res(0), per_res(1), per_res(2), per_res(3),
                  _resident(fc.shape), _resident(fs.shape), _resident(cs.shape)],
        out_specs=pl.BlockSpec((1, S, G), lambda b, g: (b, 0, g)),
        out_shape=jax.ShapeDtypeStruct((B, S, D), BF16),
        compiler_params=_params("parallel", "parallel"),
        name="fourier",
    )(h4, h4, h4, h4, fc, fs, cs)


FFN_ROW_SPLITS = 2
FFN_COL_CHUNK = 256


def _post_ffn_kernel(emit_next, x_ref, a_ref, b_ref, wa_ref, wb_ref, gpost_ref, gpre_ref, wg_ref, wu_ref,
                     wd_ref, gfpost_ref, gnext_ref, pick_ref, o_ref, *rest):
    act_ref = rest[-1]
    tm = x_ref.shape[0]
    dff = wg_ref.shape[1]
    groups = [slice(s * (tm // FFN_ROW_SPLITS), (s + 1) * (tm // FFN_ROW_SPLITS)) for s in range(FFN_ROW_SPLITS)]

    def mix(rows):
        return (jnp.dot(a_ref[rows, :], wa_ref[...], preferred_element_type=F32)
                + jnp.dot(b_ref[rows, :], wb_ref[...], preferred_element_type=F32))

    def gate_up(rows, h):
        for c in range(dff // FFN_COL_CHUNK):
            cols = slice(c * FFN_COL_CHUNK, (c + 1) * FFN_COL_CHUNK)
            g = jnp.dot(h, wg_ref[:, cols], preferred_element_type=F32)
            u = jnp.dot(h, wu_ref[:, cols], preferred_element_type=F32)
            half_g = 0.5 * g
            act_ref[rows, cols] = ((half_g * jnp.tanh(half_g) + half_g) * u).astype(BF16)

    mixed = [mix(rows) for rows in groups]
    xs = []
    for rows, m in zip(groups, mixed):
        x = x_ref[rows, :] + _rms(m, gpost_ref[...])
        xs.append(x)
        gate_up(rows, _rms(x, gpre_ref[...]).astype(BF16))
    normed = []
    for rows, x in zip(groups, xs):
        y = jnp.dot(act_ref[rows, :], wd_ref[...], preferred_element_type=F32)
        x = x + _rms(y, gfpost_ref[...])
        o_ref[rows, :] = x
        if emit_next:
            normed.append(_rms(x, gnext_ref[...]).astype(BF16))
    for s, hn in enumerate(normed):
        hn_ref = rest[0]
        n = hn.shape[0] // FOURIER_RADIX
        for r in range(FOURIER_RADIX):
            picked = jnp.dot(pick_ref[r], hn, preferred_element_type=F32)
            hn_ref[0, r, s * n:(s + 1) * n, :] = picked.astype(hn_ref.dtype)


def _post_ffn(x, a, a_col, b, b_col, w_mix, g_post, g_pre, wg, wu, wd, g_fpost, g_next, seq, tm=512):
    T, D = x.shape
    half = w_mix.shape[0] // 2
    dff = wg.shape[1]
    emit_next = g_next is not None
    rows_per_group = tm // FFN_ROW_SPLITS
    n_pick = rows_per_group // FOURIER_RADIX
    pick = np.zeros((FOURIER_RADIX, n_pick, rows_per_group), np.float32)
    for r in range(FOURIER_RADIX):
        pick[r, np.arange(n_pick), FOURIER_RADIX * np.arange(n_pick) + r] = 1.0
    pick = jnp.asarray(pick, BF16)
    row = lambda i: (i, 0)
    vec = lambda g: g.reshape(1, D)
    out_specs = [pl.BlockSpec((tm, D), row)]
    out_shape = [jax.ShapeDtypeStruct((T, D), F32)]
    if emit_next:
        steps = seq // tm
        out_specs.append(pl.BlockSpec((1, FOURIER_RADIX, tm // FOURIER_RADIX, D),
                                      lambda i: (i // steps, 0, i % steps, 0)))
        out_shape.append(jax.ShapeDtypeStruct((T // seq, FOURIER_RADIX, seq // FOURIER_RADIX, D), BF16))
    outs = pl.pallas_call(
        functools.partial(_post_ffn_kernel, emit_next),
        grid=(T // tm,),
        in_specs=[
            pl.BlockSpec((tm, D), row),
            pl.BlockSpec((tm, half), lambda i: (i, a_col)),
            pl.BlockSpec((tm, half), lambda i: (i, b_col)),
            pl.BlockSpec((half, D), lambda i: (0, 0), pipeline_mode=pl.Buffered(1)),
            pl.BlockSpec((half, D), lambda i: (1, 0), pipeline_mode=pl.Buffered(1)),
            _resident((1, D)), _resident((1, D)),
            _resident(wg.shape), _resident(wu.shape), _resident(wd.shape),
            _resident((1, D)), _resident((1, D)), _resident(pick.shape),
        ],
        out_specs=out_specs,
        out_shape=out_shape,
        scratch_shapes=[pltpu.VMEM((tm, dff), BF16)],
        compiler_params=_params("parallel"),
        name="post_ffn",
    )(x, a, b, w_mix, w_mix, vec(g_post), vec(g_pre), wg, wu, wd, vec(g_fpost),
      vec(g_next if emit_next else g_fpost), pick)
    return (outs[0], outs[1]) if emit_next else (outs[0], None)


def kernel(x, ln_mix_pre, ln_mix_post, ln_ffn_pre, ln_ffn_post, w_in, w_mix_out, lambda_q1, lambda_k1,
           lambda_q2, lambda_k2, attn_subln, conv_w, conv_b, lru_w_a, lru_b_a, lru_w_i, lru_b_i, lru_lambda,
           w_fourier_out, w_ffn_gate, w_ffn_up, w_ffn_down):
    B, S, D = x.shape
    T = B * S
    depth = ln_mix_pre.shape[0]
    heads = jnp.arange(1, ATTN_HEADS + 1, dtype=F32)
    slopes = jnp.broadcast_to(jnp.exp2(-8.0 * heads / ATTN_HEADS)[:, None, None], (ATTN_HEADS, 1, HEAD_W))
    xf = x.reshape(T, D)
    h4 = None
    for l in range(depth):
        if l % 2 == 0:
            e = l // 2
            lam_init = 0.8 - 0.6 * float(np.exp(-0.3 * l))
            q, k, v, xr, yg = _inproj(xf, ln_mix_pre[l], w_in[e].astype(BF16))
            width = q.shape[1]
            to_seq = lambda a: a.reshape(B, S, width)
            attn = _attention(to_seq(q), to_seq(k), to_seq(v), slopes, lambda_q1[e], lambda_k1[e],
                              lambda_q2[e], lambda_k2[e], attn_subln[e], lam_init)
            rec = _lru(to_seq(xr), to_seq(yg), conv_w[e], conv_b[e], lru_w_a[e], lru_b_a[e],
                       lru_w_i[e], lru_b_i[e], lru_lambda[e])
            a, a_col = attn.reshape(T, width), 0
            b, b_col = rec.reshape(T, width), 0
            w_mix = w_mix_out[e]
        else:
            f = _fourier(h4).reshape(T, D)
            a, a_col, b, b_col = f, 0, f, 1
            w_mix = w_fourier_out[l // 2]
        next_is_odd = l + 1 < depth and (l + 1) % 2 == 1
        xf, h4 = _post_ffn(xf, a, a_col, b, b_col, w_mix.astype(BF16), ln_mix_post[l], ln_ffn_pre[l],
                           w_ffn_gate[l].astype(BF16), w_ffn_up[l].astype(BF16), w_ffn_down[l].astype(BF16),
                           ln_ffn_post[l], ln_mix_pre[l + 1] if next_is_odd else None, S)
    return xf.reshape(B, S, D)
```

```python
import functools

import numpy as np
import jax
import jax.numpy as jnp
from jax import lax
from jax.experimental import pallas as pl
from jax.experimental.pallas import tpu as pltpu

F32 = jnp.float32
BF16 = jnp.bfloat16

EPS = 1e-6
ATTN_HEADS = 4
QK_DIM = 64
HEAD_W = 2 * QK_DIM
LRU_HEADS = 4
LRU_HEAD_DIM = 128
CONV_WIDTH = 4
LRU_C = 8.0
FOURIER_GROUPS = 4
FOURIER_RADIX = 4

VMEM_LIMIT_BYTES = 54 * 1024 * 1024
POS_SPLIT = 64


def _rms(x, g):
    ms = jnp.mean(x * x, axis=-1, keepdims=True)
    return x * lax.rsqrt(ms + EPS) * g


def _params(*semantics, flags=None):
    return pltpu.CompilerParams(dimension_semantics=semantics, vmem_limit_bytes=VMEM_LIMIT_BYTES, flags=flags)


def _resident(shape):
    nd = len(shape)
    return pl.BlockSpec(shape, lambda *_: (0,) * nd, pipeline_mode=pl.Buffered(1))


def _inproj_kernel(x_ref, g_ref, w_ref, q_ref, k_ref, v_ref, xr_ref, yg_ref):
    h = _rms(x_ref[...], g_ref[...]).astype(BF16)
    width = q_ref.shape[1]
    for n, o_ref in enumerate((q_ref, k_ref, v_ref, xr_ref, yg_ref)):
        z = jnp.dot(h, w_ref[:, n * width:(n + 1) * width], preferred_element_type=F32)
        if n == 0:
            z = z * (QK_DIM ** -0.5)
        o_ref[...] = z.astype(o_ref.dtype)


def _inproj(x, g, w, tm=512):
    T, D = x.shape
    width = w.shape[1] // 5
    row = lambda i: (i, 0)
    out_shapes = [jax.ShapeDtypeStruct((T, width), dt) for dt in (BF16, BF16, BF16, F32, F32)]
    return pl.pallas_call(
        _inproj_kernel,
        grid=(T // tm,),
        in_specs=[pl.BlockSpec((tm, D), row), _resident((1, D)), _resident(w.shape)],
        out_specs=[pl.BlockSpec((tm, width), row) for _ in out_shapes],
        out_shape=out_shapes,
        compiler_params=_params("parallel"),
        name="inproj",
    )(x, g.reshape(1, D), w)


ATTN_KEY_CHUNK = 256
ATTN_QUERY_BLOCK = 512
ATTN_BLOCKS_PER_STEP = 4
ATTN_VALUE_LAG = 1
BIAS_ROWS = 16
ATTN_SAFE_SHIFT = 30.0
NORM_SLACK = 1.0 + 2.0 ** -5


def _attn_kernel(lam_init, tq, q_ref, k_ref, v_ref, slope_ref, kpos_ref, corr_ref, lq1_ref, lk1_ref, lq2_ref,
                 lk2_ref, g_ref, o_ref, vt_ref, k2_ref, shift_ref):
    S = k_ref.shape[1]
    tk = ATTN_KEY_CHUNK
    nchunk = S // tk
    ndiag = tq // tk
    nblk = ATTN_BLOCKS_PER_STEP
    i = pl.program_id(2)
    sig_row = slope_ref[0]
    sig = sig_row[:, 0:1]

    @pl.when(i == 0)
    def _build_keys_values():
        row = lax.broadcasted_iota(jnp.int32, (HEAD_W, HEAD_W), 0)
        col = lax.broadcasted_iota(jnp.int32, (HEAD_W, HEAD_W), 1)
        per_map = jnp.where(col == 0, jnp.where(row < QK_DIM, 1.0, 0.0),
                            jnp.where(col == 1, jnp.where(row >= QK_DIM, 1.0, 0.0), 0.0)).astype(BF16)
        k2_max = None
        for c in range(nchunk):
            rows = slice(c * tk, (c + 1) * tk)
            kf = k_ref[0, rows, :].astype(F32)
            norms2 = jnp.dot((kf * kf).astype(BF16), per_map, preferred_element_type=F32)
            tile_max = jnp.max(norms2.reshape(tk // 8, 8, HEAD_W), axis=0)
            k2_max = tile_max if k2_max is None else jnp.maximum(k2_max, tile_max)
            vt_ref[c] = v_ref[0, rows, :].astype(F32).T.astype(BF16)
        k2_ref[...] = jnp.broadcast_to(jnp.max(k2_max, axis=0, keepdims=True) * NORM_SLACK, (8, HEAD_W))

    def k_aug(start):
        return jnp.concatenate([k_ref[0, pl.ds(start, tk), :], kpos_ref[0, pl.ds(start, tk), :]], axis=1)

    lam = (jnp.exp(jnp.sum(lq1_ref[...] * lk1_ref[...], axis=1, keepdims=True))
           - jnp.exp(jnp.sum(lq2_ref[...] * lk2_ref[...], axis=1, keepdims=True)) + lam_init)
    zero_rows = jnp.zeros((HEAD_W - BIAS_ROWS, tq), BF16)
    k2 = k2_ref[0:1, :]
    k2_maps = (k2[:, 0:1], k2[:, 1:2])
    bias_row = lax.broadcasted_iota(jnp.int32, (BIAS_ROWS, tq), 0)

    def block_setup(blk):
        first = (i * nblk + blk) * ndiag
        q_f32 = q_ref[0, blk * tq:(blk + 1) * tq, :].astype(F32).T
        q_t = q_f32.astype(BF16)
        q_row = lax.broadcasted_iota(jnp.int32, (HEAD_W, tq), 0)
        q_maps = (jnp.where(q_row < QK_DIM, q_t, jnp.zeros_like(q_t)),
                  jnp.where(q_row >= QK_DIM, q_t, jnp.zeros_like(q_t)))
        qq = q_f32 * q_f32
        bounds = [jnp.sqrt(jnp.sum(qq[m * QK_DIM:(m + 1) * QK_DIM], axis=0, keepdims=True) * k2_maps[m])
                  * NORM_SLACK for m in range(2)]
        pos = lax.broadcasted_iota(jnp.int32, (BIAS_ROWS, tq), 1) + first * tk
        p_lo = pos & (POS_SPLIT - 1)
        p_hi = pos - p_lo
        alibi = jnp.where(bias_row == 0, -sig * p_lo.astype(F32),
                          jnp.where(bias_row == 1, -sig * p_hi.astype(F32),
                                    jnp.where(bias_row < 4, 1.0, 0.0)))
        chunk_ids, starts, wrapped = [], [], []
        for r in range(nchunk):
            c = first + r
            w = c >= nchunk
            c = jnp.where(w, c - nchunk, c)
            chunk_ids.append(c)
            starts.append(pl.multiple_of(c * tk, tk))
            wrapped.append(w)
        return dict(q_maps=q_maps, bounds=bounds, alibi=alibi, chunk_ids=chunk_ids, starts=starts,
                    wrapped=wrapped)

    def operand(q_m, bias_left, bias_right, r, wrapped):
        bias = bias_left if r < ndiag else jnp.where(wrapped, bias_left, bias_right)
        return jnp.concatenate([q_m, bias, zero_rows], axis=0)

    def tile_max(s_t):
        return jnp.max(s_t.reshape(tk // 8, 8, tq), axis=0)

    setups = [block_setup(blk) for blk in range(nblk)]
    largest = None
    for blk in range(nblk):
        for m in range(2):
            bound = setups[blk]["bounds"][m]
            shift_ref[blk, m] = jnp.broadcast_to(bound, (8, tq))
            largest = bound if largest is None else jnp.maximum(largest, bound)
    bound_ok = jnp.max(largest) <= ATTN_SAFE_SHIFT

    @pl.when(jnp.logical_not(bound_ok))
    def _exact_row_max():
        for blk in range(nblk):
            st = setups[blk]
            left = st["alibi"].astype(BF16)
            right = (-st["alibi"]).astype(BF16)
            for m in range(2):
                mx = None
                for r in range(ndiag):
                    s_t = jnp.dot(k_aug(st["starts"][r]),
                                  operand(st["q_maps"][m], left, right, r, None),
                                  preferred_element_type=F32) + corr_ref[0, r]
                    mx = tile_max(s_t) if mx is None else jnp.maximum(mx, tile_max(s_t))

                def body(r, mx, blk=blk, st=st, m=m, left=left, right=right):
                    c = (i * nblk + blk) * ndiag + r
                    w = c >= nchunk
                    c = jnp.where(w, c - nchunk, c)
                    s_t = jnp.dot(k_aug(pl.multiple_of(c * tk, tk)),
                                  operand(st["q_maps"][m], left, right, ndiag, w),
                                  preferred_element_type=F32)
                    return jnp.maximum(mx, tile_max(s_t))

                mx = lax.fori_loop(ndiag, nchunk, body, mx)
                shift_ref[blk, m] = jnp.broadcast_to(jnp.max(mx, axis=0, keepdims=True), (8, tq))

    def shifted_bias(blk, m):
        neg = -shift_ref[blk, m][0:1]
        hi = neg.astype(BF16).astype(F32)
        lo = (neg - hi).astype(BF16).astype(F32)
        alibi = setups[blk]["alibi"]
        with_shift = lambda a: jnp.where(bias_row == 4, hi, jnp.where(bias_row == 5, lo, a)).astype(BF16)
        return with_shift(alibi), with_shift(-alibi)

    def finish_block(blk, acc0, acc1, sum0, sum1):
        out0 = acc0 / jnp.sum(sum0, axis=0, keepdims=True)
        out1 = acc1 / jnp.sum(sum1, axis=0, keepdims=True)
        o = (out0 - lam * out1).T
        o = _rms(o, g_ref[...]) * (1.0 - lam_init)
        o_ref[0, blk * tq:(blk + 1) * tq, :] = o.astype(o_ref.dtype)

    stream = [(blk, r) for blk in range(nblk) for r in range(nchunk)]
    biases, probs, acc, sums = {}, {}, {}, {}
    for g in range(len(stream) + ATTN_VALUE_LAG):
        cur = stream[g] if g < len(stream) else None
        old = stream[g - ATTN_VALUE_LAG] if g >= ATTN_VALUE_LAG else None
        if cur is not None and cur[1] == 0:
            biases[cur[0]] = [shifted_bias(cur[0], m) for m in range(2)]
        for m in range(2):
            if cur is not None:
                blk, r = cur
                st = setups[blk]
                left, right = biases[blk][m]
                s_t = jnp.dot(k_aug(st["starts"][r]),
                              operand(st["q_maps"][m], left, right, r, st["wrapped"][r]),
                              preferred_element_type=F32)
                if r < ndiag:
                    s_t = s_t + corr_ref[0, r]
                e_t = jnp.exp(s_t)
                probs[g, m] = e_t.astype(BF16)
                part_sum = jnp.sum(e_t.reshape(tk // 8, 8, tq), axis=0)
                sums[blk, m] = part_sum if r == 0 else sums[blk, m] + part_sum
            if old is not None:
                blk, r = old
                part = jnp.dot(vt_ref[setups[blk]["chunk_ids"][r]], probs.pop((g - ATTN_VALUE_LAG, m)),
                               preferred_element_type=F32)
                acc[blk, m] = part if r == 0 else acc[blk, m] + part
        if old is not None and old[1] == nchunk - 1:
            finish_block(old[0], acc.pop((old[0], 0)), acc.pop((old[0], 1)),
                         sums.pop((old[0], 0)), sums.pop((old[0], 1)))


def _attention(q, k, v, slopes, lq1, lk1, lq2, lk2, subln_g, lam_init):
    B, S, W = q.shape
    H = W // HEAD_W
    tk = ATTN_KEY_CHUNK
    tq = ATTN_QUERY_BLOCK
    step_q = tq * ATTN_BLOCKS_PER_STEP
    kernel = functools.partial(_attn_kernel, lam_init, tq)
    vec = lambda a: a.reshape(1, -1).astype(F32)
    small = lambda n: pl.BlockSpec((1, n), lambda b, h, i: (0, 0))
    head_slope = slopes[:, :, 0:1]
    pos = jnp.arange(S, dtype=jnp.int32)[None, :, None]
    pos_lo = (pos % POS_SPLIT).astype(F32)
    pos_hi = (pos - pos % POS_SPLIT).astype(F32)
    lane = jnp.arange(HEAD_W, dtype=jnp.int32)[None, None, :]
    kpos = jnp.where(lane < 2, 1.0, jnp.where(lane == 2, head_slope * pos_lo,
                     jnp.where(lane == 3, head_slope * pos_hi, jnp.where(lane < 6, 1.0, 0.0)))).astype(BF16)
    rel = (jnp.arange(tq, dtype=jnp.int32)[:, None] - jnp.arange(tq, dtype=jnp.int32)[None, :])
    corr = (-2.0 * slopes[:, :, 0:1]) * jnp.maximum(rel, 0).astype(F32)[None]
    corr = corr.reshape(H, tq // tk, tk, tq)
    return pl.pallas_call(
        kernel,
        grid=(B, H, S // step_q),
        in_specs=[
            pl.BlockSpec((1, step_q, HEAD_W), lambda b, h, i: (b, i, h)),
            pl.BlockSpec((1, S, HEAD_W), lambda b, h, i: (b, 0, h)),
            pl.BlockSpec((1, S, HEAD_W), lambda b, h, i: (b, 0, h)),
            pl.BlockSpec((1, 1, HEAD_W), lambda b, h, i: (h, 0, 0)),
            pl.BlockSpec((1, S, HEAD_W), lambda b, h, i: (h, 0, 0)),
            pl.BlockSpec((1, tq // tk, tk, tq), lambda b, h, i: (h, 0, 0, 0)),
            small(QK_DIM), small(QK_DIM), small(QK_DIM), small(QK_DIM), small(HEAD_W),
        ],
        out_specs=pl.BlockSpec((1, step_q, HEAD_W), lambda b, h, i: (b, i, h)),
        out_shape=jax.ShapeDtypeStruct((B, S, W), BF16),
        scratch_shapes=[
            pltpu.VMEM((S // tk, HEAD_W, tk), BF16),
            pltpu.VMEM((8, HEAD_W), F32),
            pltpu.VMEM((ATTN_BLOCKS_PER_STEP, 2, 8, tq), F32),
        ],
        compiler_params=_params("parallel", "parallel", "arbitrary"),
        name="diff_attention",
    )(q, k, v, slopes, kpos, corr, vec(lq1), vec(lk1), vec(lq2), vec(lk2), vec(subln_g))


def _scan8(a, b, reverse):
    row = lax.broadcasted_iota(jnp.int32, a.shape, 0)
    for k in (1, 2, 4):
        if reverse:
            keep = row < 8 - k
            shift = 8 - k
        else:
            keep = row >= k
            shift = k
        a_s = jnp.where(keep, pltpu.roll(a, shift, 0), 1.0)
        b_s = jnp.where(keep, pltpu.roll(b, shift, 0), 0.0)
        b = a * b_s + b
        a = a * a_s
    return a, b


GELU_C = 0.7978845608028654
GELU_CUBIC = 0.044715


def _lru_kernel(xr_ref, yg_ref, cw_ref, cb_ref, wg_ref, lam_ref, o_ref,
                xp_ref, af_ref, uf_ref, ab_ref, ub_ref):
    S = xr_ref.shape[1]
    C = LRU_HEAD_DIM
    tc = 512
    pad = 8
    xp_ref[0:pad, :] = jnp.zeros((pad, C), F32)
    xp_ref[S + pad:S + 2 * pad, :] = jnp.zeros((pad, C), F32)
    xp_ref[pad:S + pad, :] = xr_ref[0]

    lam = lam_ref[0]
    half_cls = (0.5 * LRU_C) * (jnp.minimum(lam, 0.0) - jnp.log1p(jnp.exp(-jnp.abs(lam))))
    cw = cw_ref[...]
    lane = lax.broadcasted_iota(jnp.int32, (tc, C), 1)
    ones_lanes = jnp.where(lane < 2, 1.0, 0.0).astype(BF16)
    for c in range(S // tc):
        base = c * tc
        xc = cb_ref[...]
        for t in range(CONV_WIDTH):
            off = base + pad - CONV_WIDTH // 2 + t
            xc = xc + cw[t:t + 1, :] * xp_ref[off:off + tc, :]
        lhs = jnp.concatenate([xc.astype(BF16), ones_lanes], axis=1)
        t_gates = jnp.tanh(jnp.dot(lhs, wg_ref[0], preferred_element_type=F32))
        half_x = 0.5 * xc
        for d, (a_ref, u_ref) in enumerate(((af_ref, uf_ref), (ab_ref, ub_ref))):
            t_r = t_gates[:, (2 * d) * C:(2 * d + 1) * C]
            t_i = t_gates[:, (2 * d + 1) * C:(2 * d + 2) * C]
            hc = half_cls[:, d * C:(d + 1) * C]
            log_a = t_r * hc + hc
            a = jnp.exp(log_a)
            one_minus_a2 = jnp.tanh(log_a) * (-1.0 - a * a)
            a_ref[base:base + tc, :] = a
            root = jnp.where(one_minus_a2 > 0.0, one_minus_a2 * lax.rsqrt(one_minus_a2), 0.0)
            u_ref[base:base + tc, :] = root * (t_i * half_x + half_x)

    unroll = 8
    span = 8 * unroll

    def body(it, carry):
        cf, cb = carry
        f0 = pl.multiple_of(it * span, span)
        b0 = pl.multiple_of(S - (it + 1) * span, span)
        af = af_ref[pl.ds(f0, span), :]
        uf = uf_ref[pl.ds(f0, span), :]
        hs = []
        for j in range(unroll):
            a, b = _scan8(af[8 * j:8 * j + 8], uf[8 * j:8 * j + 8], False)
            hs.append(a * cf + b)
            cf = jnp.broadcast_to(a[7:8], (8, C)) * cf + jnp.broadcast_to(b[7:8], (8, C))
        uf_ref[pl.ds(f0, span), :] = jnp.concatenate(hs, axis=0)
        ab = ab_ref[pl.ds(b0, span), :]
        ub = ub_ref[pl.ds(b0, span), :]
        hs = []
        for j in reversed(range(unroll)):
            a, b = _scan8(ab[8 * j:8 * j + 8], ub[8 * j:8 * j + 8], True)
            hs.append(a * cb + b)
            cb = jnp.broadcast_to(a[0:1], (8, C)) * cb + jnp.broadcast_to(b[0:1], (8, C))
        ub_ref[pl.ds(b0, span), :] = jnp.concatenate(hs[::-1], axis=0)
        return cf, cb

    zero = jnp.zeros((8, C), F32)
    lax.fori_loop(0, S // span, body, (zero, zero))

    for c in range(S // tc):
        rows = slice(c * tc, (c + 1) * tc)
        y = yg_ref[0, rows, :]
        half_y = 0.5 * y
        gelu = half_y * jnp.tanh(y * ((y * y) * (GELU_C * GELU_CUBIC) + GELU_C)) + half_y
        o_ref[0, rows, :] = ((uf_ref[rows, :] + ub_ref[rows, :]) * gelu).astype(o_ref.dtype)


def _lru(xr, yg, conv_w, conv_b, w_a, b_a, w_i, b_i, lam):
    B, S, W = xr.shape
    C = LRU_HEAD_DIM
    H = W // C
    per_head = lambda a: a.reshape(H, 1, C)
    wg = 0.5 * jnp.concatenate([w_a[0], w_i[0], w_a[1], w_i[1]], axis=-1)
    bg = 0.5 * jnp.concatenate([per_head(b_a[0]), per_head(b_i[0]), per_head(b_a[1]), per_head(b_i[1])], axis=-1)
    bg_hi = bg.astype(BF16)
    bg_lo = (bg - bg_hi.astype(F32)).astype(BF16)
    wg = jnp.concatenate([wg.astype(BF16), bg_hi, bg_lo, jnp.zeros((H, C - 2, 4 * C), BF16)], axis=1)
    lam2 = jnp.concatenate([per_head(lam[0]), per_head(lam[1])], axis=-1)
    seq = pl.BlockSpec((1, S, C), lambda b, h: (b, 0, h))
    return pl.pallas_call(
        _lru_kernel,
        grid=(B, H),
        in_specs=[
            seq, seq,
            pl.BlockSpec((CONV_WIDTH, C), lambda b, h: (0, h)),
            pl.BlockSpec((1, C), lambda b, h: (0, h)),
            pl.BlockSpec((1, 2 * C, 4 * C), lambda b, h: (h, 0, 0)),
            pl.BlockSpec((1, 1, 2 * C), lambda b, h: (h, 0, 0)),
        ],
        out_specs=seq,
        out_shape=jax.ShapeDtypeStruct((B, S, W), BF16),
        scratch_shapes=[pltpu.VMEM((S + 16, C), F32)] + [pltpu.VMEM((S, C), F32) for _ in range(4)],
        compiler_params=_params("parallel", "parallel"),
        name="rg_lru",
    )(xr, yg, conv_w, conv_b.reshape(1, W), wg, lam2)


@functools.lru_cache(maxsize=None)
def _dft_tables(seq, group_dim):
    n4 = seq // FOURIER_RADIX
    k1 = np.arange(n4, dtype=np.int64)[:, None]
    s1 = np.arange(n4, dtype=np.int64)[None, :]
    cos_t, sin_t = [], []
    for s2 in range(FOURIER_RADIX):
        ang = 2.0 * np.pi * ((k1 * (FOURIER_RADIX * s1 + s2)) % seq).astype(np.float64) / seq
        cos_t.append(np.cos(ang) / np.sqrt(seq))
        sin_t.append(-np.sin(ang) / np.sqrt(seq))
    c = np.arange(group_dim, dtype=np.int64)
    ang = 2.0 * np.pi * ((c[:, None] * c[None, :]) % group_dim).astype(np.float64) / group_dim
    chan = np.concatenate([np.cos(ang), np.sin(ang)], axis=0) / np.sqrt(group_dim)
    return (np.stack(cos_t).astype(np.float32), np.stack(sin_t).astype(np.float32), chan.astype(np.float32))


def _fourier_kernel(h0_ref, h1_ref, h2_ref, h3_ref, fc_ref, fs_ref, cs_ref, o_ref):
    n4 = fc_ref.shape[1]
    rows = 256
    nkc = n4 // rows

    def sequence_dft(kc):
        rs = slice(kc * rows, (kc + 1) * rows)
        yr, yi = [], []
        for r, h_ref in enumerate((h0_ref, h1_ref, h2_ref, h3_ref)):
            x = h_ref[0, 0]
            yr.append(jnp.dot(fc_ref[r, rs, :], x, preferred_element_type=F32))
            yi.append(jnp.dot(fs_ref[r, rs, :], x, preferred_element_type=F32))
        xs = (
            (yr[0] + yr[1] + yr[2] + yr[3], yi[0] + yi[1] + yi[2] + yi[3]),
            (yr[0] + yi[1] - yr[2] - yi[3], yi[0] - yr[1] - yi[2] + yr[3]),
            (yr[0] - yr[1] + yr[2] - yr[3], yi[0] - yi[1] + yi[2] - yi[3]),
            (yr[0] - yi[1] - yr[2] + yi[3], yi[0] + yr[1] - yi[2] - yr[3]),
        )
        return [jnp.concatenate([xr.astype(BF16), xi.astype(BF16)], axis=1) for xr, xi in xs]

    def channel_dft(kc, zs):
        for k2, z in enumerate(zs):
            f = jnp.dot(z, cs_ref[...], preferred_element_type=F32)
            o_ref[0, k2 * n4 + kc * rows:k2 * n4 + (kc + 1) * rows, :] = f.astype(o_ref.dtype)

    pending = None
    for kc in range(nkc):
        zs = sequence_dft(kc)
        if pending is not None:
            channel_dft(kc - 1, pending)
        pending = zs
    channel_dft(nkc - 1, pending)


def _fourier(h4):
    B, _, n4, D = h4.shape
    S = n4 * FOURIER_RADIX
    G = D // FOURIER_GROUPS
    fc, fs, cs = (jnp.asarray(t).astype(BF16) for t in _dft_tables(S, G))
    per_res = lambda r: pl.BlockSpec((1, 1, n4, G), lambda b, g: (b, r, 0, g))
    return pl.pallas_call(
        _fourier_kernel,
        grid=(B, FOURIER_GROUPS),
        in_specs=[per_res(0), per_res(1), per_res(2), per_res(3),
                  _resident(fc.shape), _resident(fs.shape), _resident(cs.shape)],
        out_specs=pl.BlockSpec((1, S, G), lambda b, g: (b, 0, g)),
        out_shape=jax.ShapeDtypeStruct((B, S, D), BF16),
        compiler_params=_params("parallel", "parallel"),
        name="fourier",
    )(h4, h4, h4, h4, fc, fs, cs)


FFN_ROW_SPLITS = 2
FFN_COL_CHUNK = 256


def _post_ffn_kernel(emit_next, x_ref, a_ref, b_ref, wa_ref, wb_ref, gpost_ref, gpre_ref, wg_ref, wu_ref,
                     wd_ref, gfpost_ref, gnext_ref, pick_ref, o_ref, *rest):
    act_ref = rest[-1]
    tm = x_ref.shape[0]
    dff = wg_ref.shape[1]
    groups = [slice(s * (tm // FFN_ROW_SPLITS), (s + 1) * (tm // FFN_ROW_SPLITS)) for s in range(FFN_ROW_SPLITS)]

    def mix(rows):
        return (jnp.dot(a_ref[rows, :], wa_ref[...], preferred_element_type=F32)
                + jnp.dot(b_ref[rows, :], wb_ref[...], preferred_element_type=F32))

    def gate_up(rows, h):
        for c in range(dff // FFN_COL_CHUNK):
            cols = slice(c * FFN_COL_CHUNK, (c + 1) * FFN_COL_CHUNK)
            g = jnp.dot(h, wg_ref[:, cols], preferred_element_type=F32)
            u = jnp.dot(h, wu_ref[:, cols], preferred_element_type=F32)
            half_g = 0.5 * g
            act_ref[rows, cols] = ((half_g * jnp.tanh(half_g) + half_g) * u).astype(BF16)

    mixed = [mix(rows) for rows in groups]
    xs = []
    for rows, m in zip(groups, mixed):
        x = x_ref[rows, :] + _rms(m, gpost_ref[...])
        xs.append(x)
        gate_up(rows, _rms(x, gpre_ref[...]).astype(BF16))
    normed = []
    for rows, x in zip(groups, xs):
        y = jnp.dot(act_ref[rows, :], wd_ref[...], preferred_element_type=F32)
        x = x + _rms(y, gfpost_ref[...])
        o_ref[rows, :] = x
        if emit_next:
            normed.append(_rms(x, gnext_ref[...]).astype(BF16))
    for s, hn in enumerate(normed):
        hn_ref = rest[0]
        n = hn.shape[0] // FOURIER_RADIX
        picked = jnp.dot(pick_ref[...], hn, preferred_element_type=F32).astype(hn_ref.dtype)
        for r in range(FOURIER_RADIX):
            hn_ref[0, r, s * n:(s + 1) * n, :] = picked[r * n:(r + 1) * n]


def _post_ffn(x, a, a_col, b, b_col, w_mix, g_post, g_pre, wg, wu, wd, g_fpost, g_next, seq, tm=512):
    T, D = x.shape
    half = w_mix.shape[0] // 2
    dff = wg.shape[1]
    emit_next = g_next is not None
    rows_per_group = tm // FFN_ROW_SPLITS
    n_pick = rows_per_group // FOURIER_RADIX
    pick = np.zeros((FOURIER_RADIX, n_pick, rows_per_group), np.float32)
    for r in range(FOURIER_RADIX):
        pick[r, np.arange(n_pick), FOURIER_RADIX * np.arange(n_pick) + r] = 1.0
    pick = jnp.asarray(pick.reshape(rows_per_group, rows_per_group), BF16)
    row = lambda i: (i, 0)
    vec = lambda g: g.reshape(1, D)
    out_specs = [pl.BlockSpec((tm, D), row)]
    out_shape = [jax.ShapeDtypeStruct((T, D), F32)]
    if emit_next:
        steps = seq // tm
        out_specs.append(pl.BlockSpec((1, FOURIER_RADIX, tm // FOURIER_RADIX, D),
                                      lambda i: (i // steps, 0, i % steps, 0)))
        out_shape.append(jax.ShapeDtypeStruct((T // seq, FOURIER_RADIX, seq // FOURIER_RADIX, D), BF16))
    outs = pl.pallas_call(
        functools.partial(_post_ffn_kernel, emit_next),
        grid=(T // tm,),
        in_specs=[
            pl.BlockSpec((tm, D), row),
            pl.BlockSpec((tm, half), lambda i: (i, a_col)),
            pl.BlockSpec((tm, half), lambda i: (i, b_col)),
            pl.BlockSpec((half, D), lambda i: (0, 0), pipeline_mode=pl.Buffered(1)),
            pl.BlockSpec((half, D), lambda i: (1, 0), pipeline_mode=pl.Buffered(1)),
            _resident((1, D)), _resident((1, D)),
            _resident(wg.shape), _resident(wu.shape), _resident(wd.shape),
            _resident((1, D)), _resident((1, D)), _resident(pick.shape),
        ],
        out_specs=out_specs,
        out_shape=out_shape,
        scratch_shapes=[pltpu.VMEM((tm, dff), BF16)],
        compiler_params=_params("parallel"),
        name="post_ffn",
    )(x, a, b, w_mix, w_mix, vec(g_post), vec(g_pre), wg, wu, wd, vec(g_fpost),
      vec(g_next if emit_next else g_fpost), pick)
    return (outs[0], outs[1]) if emit_next else (outs[0], None)


def kernel(x, ln_mix_pre, ln_mix_post, ln_ffn_pre, ln_ffn_post, w_in, w_mix_out, lambda_q1, lambda_k1,
           lambda_q2, lambda_k2, attn_subln, conv_w, conv_b, lru_w_a, lru_b_a, lru_w_i, lru_b_i, lru_lambda,
           w_fourier_out, w_ffn_gate, w_ffn_up, w_ffn_down):
    B, S, D = x.shape
    T = B * S
    depth = ln_mix_pre.shape[0]
    heads = jnp.arange(1, ATTN_HEADS + 1, dtype=F32)
    slopes = jnp.broadcast_to(jnp.exp2(-8.0 * heads / ATTN_HEADS)[:, None, None], (ATTN_HEADS, 1, HEAD_W))
    xf = x.reshape(T, D)
    h4 = None
    for l in range(depth):
        if l % 2 == 0:
            e = l // 2
            lam_init = 0.8 - 0.6 * float(np.exp(-0.3 * l))
            q, k, v, xr, yg = _inproj(xf, ln_mix_pre[l], w_in[e].astype(BF16))
            width = q.shape[1]
            to_seq = lambda a: a.reshape(B, S, width)
            attn = _attention(to_seq(q), to_seq(k), to_seq(v), slopes, lambda_q1[e], lambda_k1[e],
                              lambda_q2[e], lambda_k2[e], attn_subln[e], lam_init)
            rec = _lru(to_seq(xr), to_seq(yg), conv_w[e], conv_b[e], lru_w_a[e], lru_b_a[e],
                       lru_w_i[e], lru_b_i[e], lru_lambda[e])
            a, a_col = attn.reshape(T, width), 0
            b, b_col = rec.reshape(T, width), 0
            w_mix = w_mix_out[e]
        else:
            f = _fourier(h4).reshape(T, D)
            a, a_col, b, b_col = f, 0, f, 1
            w_mix = w_fourier_out[l // 2]
        next_is_odd = l + 1 < depth and (l + 1) % 2 == 1
        xf, h4 = _post_ffn(xf, a, a_col, b, b_col, w_mix.astype(BF16), ln_mix_post[l], ln_ffn_pre[l],
                           w_ffn_gate[l].astype(BF16), w_ffn_up[l].astype(BF16), w_ffn_down[l].astype(BF16),
                           ln_ffn_post[l], ln_mix_pre[l + 1] if next_is_odd else None, S)
    return xf.reshape(B, S, D)
```

```python
import functools

import numpy as np
import jax
import jax.numpy as jnp
from jax import lax
from jax.experimental import pallas as pl
from jax.experimental.pallas import tpu as pltpu

F32 = jnp.float32
BF16 = jnp.bfloat16

EPS = 1e-6
ATTN_HEADS = 4
QK_DIM = 64
HEAD_W = 2 * QK_DIM
LRU_HEADS = 4
LRU_HEAD_DIM = 128
CONV_WIDTH = 4
LRU_C = 8.0
FOURIER_GROUPS = 4
FOURIER_RADIX = 4

VMEM_LIMIT_BYTES = 54 * 1024 * 1024
POS_SPLIT = 64


def _rms(x, g):
    ms = jnp.mean(x * x, axis=-1, keepdims=True)
    return x * lax.rsqrt(ms + EPS) * g


def _params(*semantics, flags=None):
    return pltpu.CompilerParams(dimension_semantics=semantics, vmem_limit_bytes=VMEM_LIMIT_BYTES, flags=flags)


def _resident(shape):
    nd = len(shape)
    return pl.BlockSpec(shape, lambda *_: (0,) * nd, pipeline_mode=pl.Buffered(1))


def _inproj_kernel(x_ref, g_ref, w_ref, q_ref, k_ref, v_ref, xr_ref, yg_ref):
    h = _rms(x_ref[...], g_ref[...]).astype(BF16)
    width = q_ref.shape[1]
    for n, o_ref in enumerate((q_ref, k_ref, v_ref, xr_ref, yg_ref)):
        z = jnp.dot(h, w_ref[:, n * width:(n + 1) * width], preferred_element_type=F32)
        if n == 0:
            z = z * (QK_DIM ** -0.5)
        o_ref[...] = z.astype(o_ref.dtype)


def _inproj(x, g, w, tm=512):
    T, D = x.shape
    width = w.shape[1] // 5
    row = lambda i: (i, 0)
    out_shapes = [jax.ShapeDtypeStruct((T, width), dt) for dt in (BF16, BF16, BF16, F32, F32)]
    return pl.pallas_call(
        _inproj_kernel,
        grid=(T // tm,),
        in_specs=[pl.BlockSpec((tm, D), row), _resident((1, D)), _resident(w.shape)],
        out_specs=[pl.BlockSpec((tm, width), row) for _ in out_shapes],
        out_shape=out_shapes,
        compiler_params=_params("parallel"),
        name="inproj",
    )(x, g.reshape(1, D), w)


ATTN_KEY_CHUNK = 256
ATTN_QUERY_BLOCK = 512
ATTN_BLOCKS_PER_STEP = 4
ATTN_VALUE_LAG = 1
BIAS_ROWS = 16
ATTN_SAFE_SHIFT = 30.0
NORM_SLACK = 1.0 + 2.0 ** -5


def _attn_kernel(lam_init, tq, q_ref, k_ref, v_ref, slope_ref, kpos_ref, corr_ref, lq1_ref, lk1_ref, lq2_ref,
                 lk2_ref, g_ref, o_ref, vt_ref, k2_ref, shift_ref):
    S = k_ref.shape[1]
    tk = ATTN_KEY_CHUNK
    nchunk = S // tk
    ndiag = tq // tk
    nblk = ATTN_BLOCKS_PER_STEP
    i = pl.program_id(2)
    sig_row = slope_ref[0]
    sig = sig_row[:, 0:1]

    @pl.when(i == 0)
    def _build_keys_values():
        row = lax.broadcasted_iota(jnp.int32, (HEAD_W, HEAD_W), 0)
        col = lax.broadcasted_iota(jnp.int32, (HEAD_W, HEAD_W), 1)
        per_map = jnp.where(col == 0, jnp.where(row < QK_DIM, 1.0, 0.0),
                            jnp.where(col == 1, jnp.where(row >= QK_DIM, 1.0, 0.0), 0.0)).astype(BF16)
        k2_max = None
        for c in range(nchunk):
            rows = slice(c * tk, (c + 1) * tk)
            kf = k_ref[0, rows, :].astype(F32)
            norms2 = jnp.dot((kf * kf).astype(BF16), per_map, preferred_element_type=F32)
            tile_max = jnp.max(norms2.reshape(tk // 8, 8, HEAD_W), axis=0)
            k2_max = tile_max if k2_max is None else jnp.maximum(k2_max, tile_max)
            vt_ref[c] = v_ref[0, rows, :].astype(F32).T.astype(BF16)
        k2_ref[...] = jnp.broadcast_to(jnp.max(k2_max, axis=0, keepdims=True) * NORM_SLACK, (8, HEAD_W))

    def k_aug(start):
        return jnp.concatenate([k_ref[0, pl.ds(start, tk), :], kpos_ref[0, pl.ds(start, tk), :]], axis=1)

    lam = (jnp.exp(jnp.sum(lq1_ref[...] * lk1_ref[...], axis=1, keepdims=True))
           - jnp.exp(jnp.sum(lq2_ref[...] * lk2_ref[...], axis=1, keepdims=True)) + lam_init)
    zero_rows = jnp.zeros((HEAD_W - BIAS_ROWS, tq), BF16)
    k2 = k2_ref[0:1, :]
    k2_maps = (k2[:, 0:1], k2[:, 1:2])
    bias_row = lax.broadcasted_iota(jnp.int32, (BIAS_ROWS, tq), 0)

    def block_setup(blk):
        first = (i * nblk + blk) * ndiag
        q_f32 = q_ref[0, blk * tq:(blk + 1) * tq, :].astype(F32).T
        q_t = q_f32.astype(BF16)
        q_row = lax.broadcasted_iota(jnp.int32, (HEAD_W, tq), 0)
        q_maps = (jnp.where(q_row < QK_DIM, q_t, jnp.zeros_like(q_t)),
                  jnp.where(q_row >= QK_DIM, q_t, jnp.zeros_like(q_t)))
        qq = q_f32 * q_f32
        bounds = [jnp.sqrt(jnp.sum(qq[m * QK_DIM:(m + 1) * QK_DIM], axis=0, keepdims=True) * k2_maps[m])
                  * NORM_SLACK for m in range(2)]
        pos = lax.broadcasted_iota(jnp.int32, (BIAS_ROWS, tq), 1) + first * tk
        p_lo = pos & (POS_SPLIT - 1)
        p_hi = pos - p_lo
        alibi = jnp.where(bias_row == 0, -sig * p_lo.astype(F32),
                          jnp.where(bias_row == 1, -sig * p_hi.astype(F32),
                                    jnp.where(bias_row < 4, 1.0, 0.0)))
        chunk_ids, starts, wrapped = [], [], []
        for r in range(nchunk):
            c = first + r
            w = c >= nchunk
            c = jnp.where(w, c - nchunk, c)
            chunk_ids.append(c)
            starts.append(pl.multiple_of(c * tk, tk))
            wrapped.append(w)
        return dict(q_maps=q_maps, bounds=bounds, alibi=alibi, chunk_ids=chunk_ids, starts=starts,
                    wrapped=wrapped)

    def operand(q_m, bias_left, bias_right, r, wrapped):
        bias = bias_left if r < ndiag else jnp.where(wrapped, bias_left, bias_right)
        return jnp.concatenate([q_m, bias, zero_rows], axis=0)

    def tile_max(s_t):
        return jnp.max(s_t.reshape(tk // 8, 8, tq), axis=0)

    setups = [block_setup(blk) for blk in range(nblk)]
    largest = None
    for blk in range(nblk):
        for m in range(2):
            bound = setups[blk]["bounds"][m]
            shift_ref[blk, m] = jnp.broadcast_to(bound, (8, tq))
            largest = bound if largest is None else jnp.maximum(largest, bound)
    bound_ok = jnp.max(largest) <= ATTN_SAFE_SHIFT

    @pl.when(jnp.logical_not(bound_ok))
    def _exact_row_max():
        for blk in range(nblk):
            st = setups[blk]
            left = st["alibi"].astype(BF16)
            right = (-st["alibi"]).astype(BF16)
            for m in range(2):
                mx = None
                for r in range(ndiag):
                    s_t = jnp.dot(k_aug(st["starts"][r]),
                                  operand(st["q_maps"][m], left, right, r, None),
                                  preferred_element_type=F32) + corr_ref[0, r]
                    mx = tile_max(s_t) if mx is None else jnp.maximum(mx, tile_max(s_t))

                def body(r, mx, blk=blk, st=st, m=m, left=left, right=right):
                    c = (i * nblk + blk) * ndiag + r
                    w = c >= nchunk
                    c = jnp.where(w, c - nchunk, c)
                    s_t = jnp.dot(k_aug(pl.multiple_of(c * tk, tk)),
                                  operand(st["q_maps"][m], left, right, ndiag, w),
                                  preferred_element_type=F32)
                    return jnp.maximum(mx, tile_max(s_t))

                mx = lax.fori_loop(ndiag, nchunk, body, mx)
                shift_ref[blk, m] = jnp.broadcast_to(jnp.max(mx, axis=0, keepdims=True), (8, tq))

    def shifted_bias(blk, m):
        neg = -shift_ref[blk, m][0:1]
        hi = neg.astype(BF16).astype(F32)
        lo = (neg - hi).astype(BF16).astype(F32)
        alibi = setups[blk]["alibi"]
        with_shift = lambda a: jnp.where(bias_row == 4, hi, jnp.where(bias_row == 5, lo, a)).astype(BF16)
        return with_shift(alibi), with_shift(-alibi)

    def finish_block(blk, acc0, acc1, sum0, sum1):
        out0 = acc0 / jnp.sum(sum0, axis=0, keepdims=True)
        out1 = acc1 / jnp.sum(sum1, axis=0, keepdims=True)
        o = (out0 - lam * out1).T
        o = _rms(o, g_ref[...]) * (1.0 - lam_init)
        o_ref[0, blk * tq:(blk + 1) * tq, :] = o.astype(o_ref.dtype)

    stream = [(blk, r) for blk in range(nblk) for r in range(nchunk)]
    biases, probs, acc, sums = {}, {}, {}, {}
    for g in range(len(stream) + ATTN_VALUE_LAG):
        cur = stream[g] if g < len(stream) else None
        old = stream[g - ATTN_VALUE_LAG] if g >= ATTN_VALUE_LAG else None
        if cur is not None and cur[1] == 0:
            biases[cur[0]] = [shifted_bias(cur[0], m) for m in range(2)]
        for m in range(2):
            if cur is not None:
                blk, r = cur
                st = setups[blk]
                left, right = biases[blk][m]
                s_t = jnp.dot(k_aug(st["starts"][r]),
                              operand(st["q_maps"][m], left, right, r, st["wrapped"][r]),
                              preferred_element_type=F32)
                if r < ndiag:
                    s_t = s_t + corr_ref[0, r]
                e_t = jnp.exp(s_t)
                probs[g, m] = e_t.astype(BF16)
                part_sum = jnp.sum(e_t.reshape(tk // 8, 8, tq), axis=0)
                sums[blk, m] = part_sum if r == 0 else sums[blk, m] + part_sum
            if old is not None:
                blk, r = old
                part = jnp.dot(vt_ref[setups[blk]["chunk_ids"][r]], probs.pop((g - ATTN_VALUE_LAG, m)),
                               preferred_element_type=F32)
                acc[blk, m] = part if r == 0 else acc[blk, m] + part
        if old is not None and old[1] == nchunk - 1:
            finish_block(old[0], acc.pop((old[0], 0)), acc.pop((old[0], 1)),
                         sums.pop((old[0], 0)), sums.pop((old[0], 1)))


def _attention(q, k, v, slopes, lq1, lk1, lq2, lk2, subln_g, lam_init):
    B, S, W = q.shape
    H = W // HEAD_W
    tk = ATTN_KEY_CHUNK
    tq = ATTN_QUERY_BLOCK
    step_q = tq * ATTN_BLOCKS_PER_STEP
    kernel = functools.partial(_attn_kernel, lam_init, tq)
    vec = lambda a: a.reshape(1, -1).astype(F32)
    small = lambda n: pl.BlockSpec((1, n), lambda b, h, i: (0, 0))
    head_slope = slopes[:, :, 0:1]
    pos = jnp.arange(S, dtype=jnp.int32)[None, :, None]
    pos_lo = (pos % POS_SPLIT).astype(F32)
    pos_hi = (pos - pos % POS_SPLIT).astype(F32)
    lane = jnp.arange(HEAD_W, dtype=jnp.int32)[None, None, :]
    kpos = jnp.where(lane < 2, 1.0, jnp.where(lane == 2, head_slope * pos_lo,
                     jnp.where(lane == 3, head_slope * pos_hi, jnp.where(lane < 6, 1.0, 0.0)))).astype(BF16)
    rel = (jnp.arange(tq, dtype=jnp.int32)[:, None] - jnp.arange(tq, dtype=jnp.int32)[None, :])
    corr = (-2.0 * slopes[:, :, 0:1]) * jnp.maximum(rel, 0).astype(F32)[None]
    corr = corr.reshape(H, tq // tk, tk, tq)
    return pl.pallas_call(
        kernel,
        grid=(B, H, S // step_q),
        in_specs=[
            pl.BlockSpec((1, step_q, HEAD_W), lambda b, h, i: (b, i, h)),
            pl.BlockSpec((1, S, HEAD_W), lambda b, h, i: (b, 0, h)),
            pl.BlockSpec((1, S, HEAD_W), lambda b, h, i: (b, 0, h)),
            pl.BlockSpec((1, 1, HEAD_W), lambda b, h, i: (h, 0, 0)),
            pl.BlockSpec((1, S, HEAD_W), lambda b, h, i: (h, 0, 0)),
            pl.BlockSpec((1, tq // tk, tk, tq), lambda b, h, i: (h, 0, 0, 0)),
            small(QK_DIM), small(QK_DIM), small(QK_DIM), small(QK_DIM), small(HEAD_W),
        ],
        out_specs=pl.BlockSpec((1, step_q, HEAD_W), lambda b, h, i: (b, i, h)),
        out_shape=jax.ShapeDtypeStruct((B, S, W), BF16),
        scratch_shapes=[
            pltpu.VMEM((S // tk, HEAD_W, tk), BF16),
            pltpu.VMEM((8, HEAD_W), F32),
            pltpu.VMEM((ATTN_BLOCKS_PER_STEP, 2, 8, tq), F32),
        ],
        compiler_params=_params("parallel", "parallel", "arbitrary"),
        name="diff_attention",
    )(q, k, v, slopes, kpos, corr, vec(lq1), vec(lk1), vec(lq2), vec(lk2), vec(subln_g))


def _scan8(a, b, reverse):
    row = lax.broadcasted_iota(jnp.int32, a.shape, 0)
    for k in (1, 2, 4):
        if reverse:
            keep = row < 8 - k
            shift = 8 - k
        else:
            keep = row >= k
            shift = k
        a_s = jnp.where(keep, pltpu.roll(a, shift, 0), 1.0)
        b_s = jnp.where(keep, pltpu.roll(b, shift, 0), 0.0)
        b = a * b_s + b
        a = a * a_s
    return a, b


GELU_C = 0.7978845608028654
GELU_CUBIC = 0.044715


def _lru_kernel(xr_ref, yg_ref, cw_ref, cb_ref, wg_ref, lam_ref, o_ref,
                xp_ref, af_ref, uf_ref, ab_ref, ub_ref):
    S = xr_ref.shape[1]
    C = LRU_HEAD_DIM
    tc = 512
    pad = 8
    xp_ref[0:pad, :] = jnp.zeros((pad, C), F32)
    xp_ref[S + pad:S + 2 * pad, :] = jnp.zeros((pad, C), F32)
    xp_ref[pad:S + pad, :] = xr_ref[0]

    lam = lam_ref[0]
    half_cls = (0.5 * LRU_C) * (jnp.minimum(lam, 0.0) - jnp.log1p(jnp.exp(-jnp.abs(lam))))
    cw = cw_ref[...]
    lane = lax.broadcasted_iota(jnp.int32, (tc, C), 1)
    ones_lanes = jnp.where(lane < 2, 1.0, 0.0).astype(BF16)
    for c in range(S // tc):
        base = c * tc
        xc = cb_ref[...]
        for t in range(CONV_WIDTH):
            off = base + pad - CONV_WIDTH // 2 + t
            xc = xc + cw[t:t + 1, :] * xp_ref[off:off + tc, :]
        lhs = jnp.concatenate([xc.astype(BF16), ones_lanes], axis=1)
        t_gates = jnp.tanh(jnp.dot(lhs, wg_ref[0], preferred_element_type=F32))
        half_x = 0.5 * xc
        for d, (a_ref, u_ref) in enumerate(((af_ref, uf_ref), (ab_ref, ub_ref))):
            t_r = t_gates[:, (2 * d) * C:(2 * d + 1) * C]
            t_i = t_gates[:, (2 * d + 1) * C:(2 * d + 2) * C]
            hc = half_cls[:, d * C:(d + 1) * C]
            log_a = t_r * hc + hc
            a = jnp.exp(log_a)
            one_minus_a2 = jnp.tanh(log_a) * (-1.0 - a * a)
            a_ref[base:base + tc, :] = a
            root = jnp.where(one_minus_a2 > 0.0, one_minus_a2 * lax.rsqrt(one_minus_a2), 0.0)
            u_ref[base:base + tc, :] = root * (t_i * half_x + half_x)

    unroll = 8
    span = 8 * unroll

    def body(it, carry):
        cf, cb = carry
        f0 = pl.multiple_of(it * span, span)
        b0 = pl.multiple_of(S - (it + 1) * span, span)
        af = af_ref[pl.ds(f0, span), :]
        uf = uf_ref[pl.ds(f0, span), :]
        hs = []
        for j in range(unroll):
            a, b = _scan8(af[8 * j:8 * j + 8], uf[8 * j:8 * j + 8], False)
            hs.append(a * cf + b)
            cf = jnp.broadcast_to(a[7:8], (8, C)) * cf + jnp.broadcast_to(b[7:8], (8, C))
        uf_ref[pl.ds(f0, span), :] = jnp.concatenate(hs, axis=0)
        ab = ab_ref[pl.ds(b0, span), :]
        ub = ub_ref[pl.ds(b0, span), :]
        hs = []
        for j in reversed(range(unroll)):
            a, b = _scan8(ab[8 * j:8 * j + 8], ub[8 * j:8 * j + 8], True)
            hs.append(a * cb + b)
            cb = jnp.broadcast_to(a[0:1], (8, C)) * cb + jnp.broadcast_to(b[0:1], (8, C))
        ub_ref[pl.ds(b0, span), :] = jnp.concatenate(hs[::-1], axis=0)
        return cf, cb

    zero = jnp.zeros((8, C), F32)
    lax.fori_loop(0, S // span, body, (zero, zero))

    for c in range(S // tc):
        rows = slice(c * tc, (c + 1) * tc)
        y = yg_ref[0, rows, :]
        half_y = 0.5 * y
        gelu = half_y * jnp.tanh(y * ((y * y) * (GELU_C * GELU_CUBIC) + GELU_C)) + half_y
        o_ref[0, rows, :] = ((uf_ref[rows, :] + ub_ref[rows, :]) * gelu).astype(o_ref.dtype)


def _lru(xr, yg, conv_w, conv_b, w_a, b_a, w_i, b_i, lam):
    B, S, W = xr.shape
    C = LRU_HEAD_DIM
    H = W // C
    per_head = lambda a: a.reshape(H, 1, C)
    wg = 0.5 * jnp.concatenate([w_a[0], w_i[0], w_a[1], w_i[1]], axis=-1)
    bg = 0.5 * jnp.concatenate([per_head(b_a[0]), per_head(b_i[0]), per_head(b_a[1]), per_head(b_i[1])], axis=-1)
    bg_hi = bg.astype(BF16)
    bg_lo = (bg - bg_hi.astype(F32)).astype(BF16)
    wg = jnp.concatenate([wg.astype(BF16), bg_hi, bg_lo, jnp.zeros((H, C - 2, 4 * C), BF16)], axis=1)
    lam2 = jnp.concatenate([per_head(lam[0]), per_head(lam[1])], axis=-1)
    seq = pl.BlockSpec((1, S, C), lambda b, h: (b, 0, h))
    return pl.pallas_call(
        _lru_kernel,
        grid=(B, H),
        in_specs=[
            seq, seq,
            pl.BlockSpec((CONV_WIDTH, C), lambda b, h: (0, h)),
            pl.BlockSpec((1, C), lambda b, h: (0, h)),
            pl.BlockSpec((1, 2 * C, 4 * C), lambda b, h: (h, 0, 0)),
            pl.BlockSpec((1, 1, 2 * C), lambda b, h: (h, 0, 0)),
        ],
        out_specs=seq,
        out_shape=jax.ShapeDtypeStruct((B, S, W), BF16),
        scratch_shapes=[pltpu.VMEM((S + 16, C), F32)] + [pltpu.VMEM((S, C), F32) for _ in range(4)],
        compiler_params=_params("parallel", "parallel"),
        name="rg_lru",
    )(xr, yg, conv_w, conv_b.reshape(1, W), wg, lam2)


@functools.lru_cache(maxsize=None)
def _dft_tables(seq, group_dim):
    n4 = seq // FOURIER_RADIX
    k1 = np.arange(n4, dtype=np.int64)[:, None]
    s1 = np.arange(n4, dtype=np.int64)[None, :]
    cos_t, sin_t = [], []
    for s2 in range(FOURIER_RADIX):
        ang = 2.0 * np.pi * ((k1 * (FOURIER_RADIX * s1 + s2)) % seq).astype(np.float64) / seq
        cos_t.append(np.cos(ang) / np.sqrt(seq))
        sin_t.append(-np.sin(ang) / np.sqrt(seq))
    c = np.arange(group_dim, dtype=np.int64)
    ang = 2.0 * np.pi * ((c[:, None] * c[None, :]) % group_dim).astype(np.float64) / group_dim
    chan = np.concatenate([np.cos(ang), np.sin(ang)], axis=0) / np.sqrt(group_dim)
    return (np.stack(cos_t).astype(np.float32), np.stack(sin_t).astype(np.float32), chan.astype(np.float32))


def _fourier_kernel(h0_ref, h1_ref, h2_ref, h3_ref, fc_ref, fs_ref, cs_ref, o_ref):
    n4 = fc_ref.shape[1]
    rows = 512
    nkc = n4 // rows

    def sequence_dft(kc):
        rs = slice(kc * rows, (kc + 1) * rows)
        yr, yi = [], []
        for r, h_ref in enumerate((h0_ref, h1_ref, h2_ref, h3_ref)):
            x = h_ref[0, 0]
            yr.append(jnp.dot(fc_ref[r, rs, :], x, preferred_element_type=F32))
            yi.append(jnp.dot(fs_ref[r, rs, :], x, preferred_element_type=F32))
        xs = (
            (yr[0] + yr[1] + yr[2] + yr[3], yi[0] + yi[1] + yi[2] + yi[3]),
            (yr[0] + yi[1] - yr[2] - yi[3], yi[0] - yr[1] - yi[2] + yr[3]),
            (yr[0] - yr[1] + yr[2] - yr[3], yi[0] - yi[1] + yi[2] - yi[3]),
            (yr[0] - yi[1] - yr[2] + yi[3], yi[0] + yr[1] - yi[2] - yr[3]),
        )
        return [jnp.concatenate([xr.astype(BF16), xi.astype(BF16)], axis=1) for xr, xi in xs]

    def channel_dft(kc, zs):
        for k2, z in enumerate(zs):
            f = jnp.dot(z, cs_ref[...], preferred_element_type=F32)
            o_ref[0, k2 * n4 + kc * rows:k2 * n4 + (kc + 1) * rows, :] = f.astype(o_ref.dtype)

    pending = None
    for kc in range(nkc):
        zs = sequence_dft(kc)
        if pending is not None:
            channel_dft(kc - 1, pending)
        pending = zs
    channel_dft(nkc - 1, pending)


def _fourier(h4):
    B, _, n4, D = h4.shape
    S = n4 * FOURIER_RADIX
    G = D // FOURIER_GROUPS
    fc, fs, cs = (jnp.asarray(t).astype(BF16) for t in _dft_tables(S, G))
    per_res = lambda r: pl.BlockSpec((1, 1, n4, G), lambda b, g: (b, r, 0, g))
    return pl.pallas_call(
        _fourier_kernel,
        grid=(B, FOURIER_GROUPS),
        in_specs=[per_res(0), per_res(1), per_res(2), per_res(3),
                  _resident(fc.shape), _resident(fs.shape), _resident(cs.shape)],
        out_specs=pl.BlockSpec((1, S, G), lambda b, g: (b, 0, g)),
        out_shape=jax.ShapeDtypeStruct((B, S, D), BF16),
        compiler_params=_params("parallel", "parallel"),
        name="fourier",
    )(h4, h4, h4, h4, fc, fs, cs)


FFN_ROW_SPLITS = 2
FFN_COL_CHUNK = 256


def _post_ffn_kernel(emit_next, x_ref, a_ref, b_ref, wa_ref, wb_ref, gpost_ref, gpre_ref, wg_ref, wu_ref,
                     wd_ref, gfpost_ref, gnext_ref, pick_ref, o_ref, *rest):
    act_ref = rest[-1]
    tm = x_ref.shape[0]
    dff = wg_ref.shape[1]
    groups = [slice(s * (tm // FFN_ROW_SPLITS), (s + 1) * (tm // FFN_ROW_SPLITS)) for s in range(FFN_ROW_SPLITS)]

    def mix(rows):
        return (jnp.dot(a_ref[rows, :], wa_ref[...], preferred_element_type=F32)
                + jnp.dot(b_ref[rows, :], wb_ref[...], preferred_element_type=F32))

    def gate_up(rows, h):
        for c in range(dff // FFN_COL_CHUNK):
            cols = slice(c * FFN_COL_CHUNK, (c + 1) * FFN_COL_CHUNK)
            g = jnp.dot(h, wg_ref[:, cols], preferred_element_type=F32)
            u = jnp.dot(h, wu_ref[:, cols], preferred_element_type=F32)
            half_g = 0.5 * g
            act_ref[rows, cols] = ((half_g * jnp.tanh(half_g) + half_g) * u).astype(BF16)

    mixed = [mix(rows) for rows in groups]
    xs = []
    for rows, m in zip(groups, mixed):
        x = x_ref[rows, :] + _rms(m, gpost_ref[...])
        xs.append(x)
        gate_up(rows, _rms(x, gpre_ref[...]).astype(BF16))
    normed = []
    for rows, x in zip(groups, xs):
        y = jnp.dot(act_ref[rows, :], wd_ref[...], preferred_element_type=F32)
        x = x + _rms(y, gfpost_ref[...])
        o_ref[rows, :] = x
        if emit_next:
            normed.append(_rms(x, gnext_ref[...]).astype(BF16))
    for s, hn in enumerate(normed):
        hn_ref = rest[0]
        n = hn.shape[0] // FOURIER_RADIX
        picked = jnp.dot(pick_ref[...], hn, preferred_element_type=F32).astype(hn_ref.dtype)
        for r in range(FOURIER_RADIX):
            hn_ref[0, r, s * n:(s + 1) * n, :] = picked[r * n:(r + 1) * n]


def _post_ffn(x, a, a_col, b, b_col, w_mix, g_post, g_pre, wg, wu, wd, g_fpost, g_next, seq, tm=512):
    T, D = x.shape
    half = w_mix.shape[0] // 2
    dff = wg.shape[1]
    emit_next = g_next is not None
    rows_per_group = tm // FFN_ROW_SPLITS
    n_pick = rows_per_group // FOURIER_RADIX
    pick = np.zeros((FOURIER_RADIX, n_pick, rows_per_group), np.float32)
    for r in range(FOURIER_RADIX):
        pick[r, np.arange(n_pick), FOURIER_RADIX * np.arange(n_pick) + r] = 1.0
    pick = jnp.asarray(pick.reshape(rows_per_group, rows_per_group), BF16)
    row = lambda i: (i, 0)
    vec = lambda g: g.reshape(1, D)
    out_specs = [pl.BlockSpec((tm, D), row)]
    out_shape = [jax.ShapeDtypeStruct((T, D), F32)]
    if emit_next:
        steps = seq // tm
        out_specs.append(pl.BlockSpec((1, FOURIER_RADIX, tm // FOURIER_RADIX, D),
                                      lambda i: (i // steps, 0, i % steps, 0)))
        out_shape.append(jax.ShapeDtypeStruct((T // seq, FOURIER_RADIX, seq // FOURIER_RADIX, D), BF16))
    outs = pl.pallas_call(
        functools.partial(_post_ffn_kernel, emit_next),
        grid=(T // tm,),
        in_specs=[
            pl.BlockSpec((tm, D), row),
            pl.BlockSpec((tm, half), lambda i: (i, a_col)),
            pl.BlockSpec((tm, half), lambda i: (i, b_col)),
            pl.BlockSpec((half, D), lambda i: (0, 0), pipeline_mode=pl.Buffered(1)),
            pl.BlockSpec((half, D), lambda i: (1, 0), pipeline_mode=pl.Buffered(1)),
            _resident((1, D)), _resident((1, D)),
            _resident(wg.shape), _resident(wu.shape), _resident(wd.shape),
            _resident((1, D)), _resident((1, D)), _resident(pick.shape),
        ],
        out_specs=out_specs,
        out_shape=out_shape,
        scratch_shapes=[pltpu.VMEM((tm, dff), BF16)],
        compiler_params=_params("parallel"),
        name="post_ffn",
    )(x, a, b, w_mix, w_mix, vec(g_post), vec(g_pre), wg, wu, wd, vec(g_fpost),
      vec(g_next if emit_next else g_fpost), pick)
    return (outs[0], outs[1]) if emit_next else (outs[0], None)


def kernel(x, ln_mix_pre, ln_mix_post, ln_ffn_pre, ln_ffn_post, w_in, w_mix_out, lambda_q1, lambda_k1,
           lambda_q2, lambda_k2, attn_subln, conv_w, conv_b, lru_w_a, lru_b_a, lru_w_i, lru_b_i, lru_lambda,
           w_fourier_out, w_ffn_gate, w_ffn_up, w_ffn_down):
    B, S, D = x.shape
    T = B * S
    depth = ln_mix_pre.shape[0]
    heads = jnp.arange(1, ATTN_HEADS + 1, dtype=F32)
    slopes = jnp.broadcast_to(jnp.exp2(-8.0 * heads / ATTN_HEADS)[:, None, None], (ATTN_HEADS, 1, HEAD_W))
    w_in, w_mix_out, w_fourier_out, w_ffn_gate, w_ffn_up, w_ffn_down = (
        w.astype(BF16) for w in (w_in, w_mix_out, w_fourier_out, w_ffn_gate, w_ffn_up, w_ffn_down))
    xf = x.reshape(T, D)
    h4 = None
    for l in range(depth):
        if l % 2 == 0:
            e = l // 2
            lam_init = 0.8 - 0.6 * float(np.exp(-0.3 * l))
            q, k, v, xr, yg = _inproj(xf, ln_mix_pre[l], w_in[e])
            width = q.shape[1]
            to_seq = lambda a: a.reshape(B, S, width)
            attn = _attention(to_seq(q), to_seq(k), to_seq(v), slopes, lambda_q1[e], lambda_k1[e],
                              lambda_q2[e], lambda_k2[e], attn_subln[e], lam_init)
            rec = _lru(to_seq(xr), to_seq(yg), conv_w[e], conv_b[e], lru_w_a[e], lru_b_a[e],
                       lru_w_i[e], lru_b_i[e], lru_lambda[e])
            a, a_col = attn.reshape(T, width), 0
            b, b_col = rec.reshape(T, width), 0
            w_mix = w_mix_out[e]
        else:
            f = _fourier(h4).reshape(T, D)
            a, a_col, b, b_col = f, 0, f, 1
            w_mix = w_fourier_out[l // 2]
        next_is_odd = l + 1 < depth and (l + 1) % 2 == 1
        xf, h4 = _post_ffn(xf, a, a_col, b, b_col, w_mix, ln_mix_post[l], ln_ffn_pre[l],
                           w_ffn_gate[l], w_ffn_up[l], w_ffn_down[l],
                           ln_ffn_post[l], ln_mix_pre[l + 1] if next_is_odd else None, S)
    return xf.reshape(B, S, D)
```

```python
import functools

import numpy as np
import jax
import jax.numpy as jnp
from jax import lax
from jax.experimental import pallas as pl
from jax.experimental.pallas import tpu as pltpu

F32 = jnp.float32
BF16 = jnp.bfloat16

EPS = 1e-6
ATTN_HEADS = 4
QK_DIM = 64
HEAD_W = 2 * QK_DIM
LRU_HEADS = 4
LRU_HEAD_DIM = 128
CONV_WIDTH = 4
LRU_C = 8.0
FOURIER_GROUPS = 4
FOURIER_RADIX = 4

VMEM_LIMIT_BYTES = 54 * 1024 * 1024
POS_SPLIT = 64


def _rms(x, g):
    ms = jnp.mean(x * x, axis=-1, keepdims=True)
    return x * lax.rsqrt(ms + EPS) * g


def _params(*semantics, flags=None):
    return pltpu.CompilerParams(dimension_semantics=semantics, vmem_limit_bytes=VMEM_LIMIT_BYTES, flags=flags)


def _resident(shape):
    nd = len(shape)
    return pl.BlockSpec(shape, lambda *_: (0,) * nd, pipeline_mode=pl.Buffered(1))


INPROJ_ROW_SPLITS = 2


def _inproj_kernel(x_ref, g_ref, w_ref, q_ref, k_ref, v_ref, xr_ref, yg_ref):
    tm = x_ref.shape[0]
    width = q_ref.shape[1]
    groups = [slice(s * (tm // INPROJ_ROW_SPLITS), (s + 1) * (tm // INPROJ_ROW_SPLITS))
              for s in range(INPROJ_ROW_SPLITS)]
    normed = [_rms(x_ref[rows, :], g_ref[...]).astype(BF16) for rows in groups]
    for rows, h in zip(groups, normed):
        for n, o_ref in enumerate((q_ref, k_ref, v_ref, xr_ref, yg_ref)):
            z = jnp.dot(h, w_ref[:, n * width:(n + 1) * width], preferred_element_type=F32)
            if n == 0:
                z = z * (QK_DIM ** -0.5)
            o_ref[rows, :] = z.astype(o_ref.dtype)


def _inproj(x, g, w, tm=1024):
    T, D = x.shape
    width = w.shape[1] // 5
    row = lambda i: (i, 0)
    out_shapes = [jax.ShapeDtypeStruct((T, width), dt) for dt in (BF16, BF16, BF16, F32, F32)]
    return pl.pallas_call(
        _inproj_kernel,
        grid=(T // tm,),
        in_specs=[pl.BlockSpec((tm, D), row), _resident((1, D)), _resident(w.shape)],
        out_specs=[pl.BlockSpec((tm, width), row) for _ in out_shapes],
        out_shape=out_shapes,
        compiler_params=_params("parallel"),
        name="inproj",
    )(x, g.reshape(1, D), w)


ATTN_KEY_CHUNK = 256
ATTN_QUERY_BLOCK = 512
ATTN_BLOCKS_PER_STEP = 4
ATTN_VALUE_LAG = 1
BIAS_ROWS = 16
ATTN_SAFE_SHIFT = 30.0
NORM_SLACK = 1.0 + 2.0 ** -5


def _attn_kernel(lam_init, tq, q_ref, k_ref, v_ref, slope_ref, kpos_ref, corr_ref, lq1_ref, lk1_ref, lq2_ref,
                 lk2_ref, g_ref, o_ref, vt_ref, k2_ref, shift_ref):
    S = k_ref.shape[1]
    tk = ATTN_KEY_CHUNK
    nchunk = S // tk
    ndiag = tq // tk
    nblk = ATTN_BLOCKS_PER_STEP
    i = pl.program_id(2)
    sig_row = slope_ref[0]
    sig = sig_row[:, 0:1]

    @pl.when(i == 0)
    def _build_keys_values():
        row = lax.broadcasted_iota(jnp.int32, (HEAD_W, HEAD_W), 0)
        col = lax.broadcasted_iota(jnp.int32, (HEAD_W, HEAD_W), 1)
        per_map = jnp.where(col == 0, jnp.where(row < QK_DIM, 1.0, 0.0),
                            jnp.where(col == 1, jnp.where(row >= QK_DIM, 1.0, 0.0), 0.0)).astype(BF16)
        k2_max = None
        for c in range(nchunk):
            rows = slice(c * tk, (c + 1) * tk)
            kf = k_ref[0, rows, :].astype(F32)
            norms2 = jnp.dot((kf * kf).astype(BF16), per_map, preferred_element_type=F32)
            tile_max = jnp.max(norms2.reshape(tk // 8, 8, HEAD_W), axis=0)
            k2_max = tile_max if k2_max is None else jnp.maximum(k2_max, tile_max)
            vt_ref[c] = v_ref[0, rows, :].astype(F32).T.astype(BF16)
        k2_ref[...] = jnp.broadcast_to(jnp.max(k2_max, axis=0, keepdims=True) * NORM_SLACK, (8, HEAD_W))

    def k_aug(start):
        return jnp.concatenate([k_ref[0, pl.ds(start, tk), :], kpos_ref[0, pl.ds(start, tk), :]], axis=1)

    lam = (jnp.exp(jnp.sum(lq1_ref[...] * lk1_ref[...], axis=1, keepdims=True))
           - jnp.exp(jnp.sum(lq2_ref[...] * lk2_ref[...], axis=1, keepdims=True)) + lam_init)
    zero_rows = jnp.zeros((HEAD_W - BIAS_ROWS, tq), BF16)
    k2 = k2_ref[0:1, :]
    k2_maps = (k2[:, 0:1], k2[:, 1:2])
    bias_row = lax.broadcasted_iota(jnp.int32, (BIAS_ROWS, tq), 0)

    def block_setup(blk):
        first = (i * nblk + blk) * ndiag
        q_f32 = q_ref[0, blk * tq:(blk + 1) * tq, :].astype(F32).T
        q_t = q_f32.astype(BF16)
        q_row = lax.broadcasted_iota(jnp.int32, (HEAD_W, tq), 0)
        q_maps = (jnp.where(q_row < QK_DIM, q_t, jnp.zeros_like(q_t)),
                  jnp.where(q_row >= QK_DIM, q_t, jnp.zeros_like(q_t)))
        qq = q_f32 * q_f32
        bounds = [jnp.sqrt(jnp.sum(qq[m * QK_DIM:(m + 1) * QK_DIM], axis=0, keepdims=True) * k2_maps[m])
                  * NORM_SLACK for m in range(2)]
        pos = lax.broadcasted_iota(jnp.int32, (BIAS_ROWS, tq), 1) + first * tk
        p_lo = pos & (POS_SPLIT - 1)
        p_hi = pos - p_lo
        alibi = jnp.where(bias_row == 0, -sig * p_lo.astype(F32),
                          jnp.where(bias_row == 1, -sig * p_hi.astype(F32),
                                    jnp.where(bias_row < 4, 1.0, 0.0)))
        chunk_ids, starts, wrapped = [], [], []
        for r in range(nchunk):
            c = first + r
            w = c >= nchunk
            c = jnp.where(w, c - nchunk, c)
            chunk_ids.append(c)
            starts.append(pl.multiple_of(c * tk, tk))
            wrapped.append(w)
        return dict(q_maps=q_maps, bounds=bounds, alibi=alibi, chunk_ids=chunk_ids, starts=starts,
                    wrapped=wrapped)

    def operand(q_m, bias_left, bias_right, r, wrapped):
        bias = bias_left if r < ndiag else jnp.where(wrapped, bias_left, bias_right)
        return jnp.concatenate([q_m, bias, zero_rows], axis=0)

    def tile_max(s_t):
        return jnp.max(s_t.reshape(tk // 8, 8, tq), axis=0)

    setups = [block_setup(blk) for blk in range(nblk)]
    largest = None
    for blk in range(nblk):
        for m in range(2):
            bound = setups[blk]["bounds"][m]
            shift_ref[blk, m] = jnp.broadcast_to(bound, (8, tq))
            largest = bound if largest is None else jnp.maximum(largest, bound)
    bound_ok = jnp.max(largest) <= ATTN_SAFE_SHIFT

    @pl.when(jnp.logical_not(bound_ok))
    def _exact_row_max():
        for blk in range(nblk):
            st = setups[blk]
            left = st["alibi"].astype(BF16)
            right = (-st["alibi"]).astype(BF16)
            for m in range(2):
                mx = None
                for r in range(ndiag):
                    s_t = jnp.dot(k_aug(st["starts"][r]),
                                  operand(st["q_maps"][m], left, right, r, None),
                                  preferred_element_type=F32) + corr_ref[0, r]
                    mx = tile_max(s_t) if mx is None else jnp.maximum(mx, tile_max(s_t))

                def body(r, mx, blk=blk, st=st, m=m, left=left, right=right):
                    c = (i * nblk + blk) * ndiag + r
                    w = c >= nchunk
                    c = jnp.where(w, c - nchunk, c)
                    s_t = jnp.dot(k_aug(pl.multiple_of(c * tk, tk)),
                                  operand(st["q_maps"][m], left, right, ndiag, w),
                                  preferred_element_type=F32)
                    return jnp.maximum(mx, tile_max(s_t))

                mx = lax.fori_loop(ndiag, nchunk, body, mx)
                shift_ref[blk, m] = jnp.broadcast_to(jnp.max(mx, axis=0, keepdims=True), (8, tq))

    def shifted_bias(blk, m):
        neg = -shift_ref[blk, m][0:1]
        hi = neg.astype(BF16).astype(F32)
        lo = (neg - hi).astype(BF16).astype(F32)
        alibi = setups[blk]["alibi"]
        with_shift = lambda a: jnp.where(bias_row == 4, hi, jnp.where(bias_row == 5, lo, a)).astype(BF16)
        return with_shift(alibi), with_shift(-alibi)

    def finish_block(blk, acc0, acc1, sum0, sum1):
        out0 = acc0 / jnp.sum(sum0, axis=0, keepdims=True)
        out1 = acc1 / jnp.sum(sum1, axis=0, keepdims=True)
        o = (out0 - lam * out1).T
        o = _rms(o, g_ref[...]) * (1.0 - lam_init)
        o_ref[0, blk * tq:(blk + 1) * tq, :] = o.astype(o_ref.dtype)

    stream = [(blk, r) for blk in range(nblk) for r in range(nchunk)]
    biases, probs, acc, sums = {}, {}, {}, {}
    for g in range(len(stream) + ATTN_VALUE_LAG):
        cur = stream[g] if g < len(stream) else None
        old = stream[g - ATTN_VALUE_LAG] if g >= ATTN_VALUE_LAG else None
        if cur is not None and cur[1] == 0:
            biases[cur[0]] = [shifted_bias(cur[0], m) for m in range(2)]
        for m in range(2):
            if cur is not None:
                blk, r = cur
                st = setups[blk]
                left, right = biases[blk][m]
                s_t = jnp.dot(k_aug(st["starts"][r]),
                              operand(st["q_maps"][m], left, right, r, st["wrapped"][r]),
                              preferred_element_type=F32)
                if r < ndiag:
                    s_t = s_t + corr_ref[0, r]
                e_t = jnp.exp(s_t)
                probs[g, m] = e_t.astype(BF16)
                part_sum = jnp.sum(e_t.reshape(tk // 8, 8, tq), axis=0)
                sums[blk, m] = part_sum if r == 0 else sums[blk, m] + part_sum
            if old is not None:
                blk, r = old
                part = jnp.dot(vt_ref[setups[blk]["chunk_ids"][r]], probs.pop((g - ATTN_VALUE_LAG, m)),
                               preferred_element_type=F32)
                acc[blk, m] = part if r == 0 else acc[blk, m] + part
        if old is not None and old[1] == nchunk - 1:
            finish_block(old[0], acc.pop((old[0], 0)), acc.pop((old[0], 1)),
                         sums.pop((old[0], 0)), sums.pop((old[0], 1)))


def _attention(q, k, v, slopes, lq1, lk1, lq2, lk2, subln_g, lam_init):
    B, S, W = q.shape
    H = W // HEAD_W
    tk = ATTN_KEY_CHUNK
    tq = ATTN_QUERY_BLOCK
    step_q = tq * ATTN_BLOCKS_PER_STEP
    kernel = functools.partial(_attn_kernel, lam_init, tq)
    vec = lambda a: a.reshape(1, -1).astype(F32)
    small = lambda n: pl.BlockSpec((1, n), lambda b, h, i: (0, 0))
    head_slope = slopes[:, :, 0:1]
    pos = jnp.arange(S, dtype=jnp.int32)[None, :, None]
    pos_lo = (pos % POS_SPLIT).astype(F32)
    pos_hi = (pos - pos % POS_SPLIT).astype(F32)
    lane = jnp.arange(HEAD_W, dtype=jnp.int32)[None, None, :]
    kpos = jnp.where(lane < 2, 1.0, jnp.where(lane == 2, head_slope * pos_lo,
                     jnp.where(lane == 3, head_slope * pos_hi, jnp.where(lane < 6, 1.0, 0.0)))).astype(BF16)
    rel = (jnp.arange(tq, dtype=jnp.int32)[:, None] - jnp.arange(tq, dtype=jnp.int32)[None, :])
    corr = (-2.0 * slopes[:, :, 0:1]) * jnp.maximum(rel, 0).astype(F32)[None]
    corr = corr.reshape(H, tq // tk, tk, tq)
    return pl.pallas_call(
        kernel,
        grid=(B, H, S // step_q),
        in_specs=[
            pl.BlockSpec((1, step_q, HEAD_W), lambda b, h, i: (b, i, h)),
            pl.BlockSpec((1, S, HEAD_W), lambda b, h, i: (b, 0, h)),
            pl.BlockSpec((1, S, HEAD_W), lambda b, h, i: (b, 0, h)),
            pl.BlockSpec((1, 1, HEAD_W), lambda b, h, i: (h, 0, 0)),
            pl.BlockSpec((1, S, HEAD_W), lambda b, h, i: (h, 0, 0)),
            pl.BlockSpec((1, tq // tk, tk, tq), lambda b, h, i: (h, 0, 0, 0)),
            small(QK_DIM), small(QK_DIM), small(QK_DIM), small(QK_DIM), small(HEAD_W),
        ],
        out_specs=pl.BlockSpec((1, step_q, HEAD_W), lambda b, h, i: (b, i, h)),
        out_shape=jax.ShapeDtypeStruct((B, S, W), BF16),
        scratch_shapes=[
            pltpu.VMEM((S // tk, HEAD_W, tk), BF16),
            pltpu.VMEM((8, HEAD_W), F32),
            pltpu.VMEM((ATTN_BLOCKS_PER_STEP, 2, 8, tq), F32),
        ],
        compiler_params=_params("parallel", "parallel", "arbitrary"),
        name="diff_attention",
    )(q, k, v, slopes, kpos, corr, vec(lq1), vec(lk1), vec(lq2), vec(lk2), vec(subln_g))


def _scan8(a, b, reverse):
    row = lax.broadcasted_iota(jnp.int32, a.shape, 0)
    for k in (1, 2, 4):
        if reverse:
            keep = row < 8 - k
            shift = 8 - k
        else:
            keep = row >= k
            shift = k
        a_s = jnp.where(keep, pltpu.roll(a, shift, 0), 1.0)
        b_s = jnp.where(keep, pltpu.roll(b, shift, 0), 0.0)
        b = a * b_s + b
        a = a * a_s
    return a, b


GELU_C = 0.7978845608028654
GELU_CUBIC = 0.044715


def _lru_kernel(xr_ref, yg_ref, cw_ref, cb_ref, wg_ref, lam_ref, o_ref,
                xp_ref, af_ref, uf_ref, ab_ref, ub_ref):
    S = xr_ref.shape[1]
    C = LRU_HEAD_DIM
    tc = 512
    pad = 8
    xp_ref[0:pad, :] = jnp.zeros((pad, C), F32)
    xp_ref[S + pad:S + 2 * pad, :] = jnp.zeros((pad, C), F32)
    xp_ref[pad:S + pad, :] = xr_ref[0]

    lam = lam_ref[0]
    half_cls = (0.5 * LRU_C) * (jnp.minimum(lam, 0.0) - jnp.log1p(jnp.exp(-jnp.abs(lam))))
    cw = cw_ref[...]
    lane = lax.broadcasted_iota(jnp.int32, (tc, C), 1)
    ones_lanes = jnp.where(lane < 2, 1.0, 0.0).astype(BF16)
    for c in range(S // tc):
        base = c * tc
        xc = cb_ref[...]
        for t in range(CONV_WIDTH):
            off = base + pad - CONV_WIDTH // 2 + t
            xc = xc + cw[t:t + 1, :] * xp_ref[off:off + tc, :]
        lhs = jnp.concatenate([xc.astype(BF16), ones_lanes], axis=1)
        t_gates = jnp.tanh(jnp.dot(lhs, wg_ref[0], preferred_element_type=F32))
        half_x = 0.5 * xc
        for d, (a_ref, u_ref) in enumerate(((af_ref, uf_ref), (ab_ref, ub_ref))):
            t_r = t_gates[:, (2 * d) * C:(2 * d + 1) * C]
            t_i = t_gates[:, (2 * d + 1) * C:(2 * d + 2) * C]
            hc = half_cls[:, d * C:(d + 1) * C]
            log_a = t_r * hc + hc
            a = jnp.exp(log_a)
            one_minus_a2 = jnp.tanh(log_a) * (-1.0 - a * a)
            a_ref[base:base + tc, :] = a
            root = jnp.where(one_minus_a2 > 0.0, one_minus_a2 * lax.rsqrt(one_minus_a2), 0.0)
            u_ref[base:base + tc, :] = root * (t_i * half_x + half_x)

    unroll = 8
    span = 8 * unroll

    def body(it, carry):
        cf, cb = carry
        f0 = pl.multiple_of(it * span, span)
        b0 = pl.multiple_of(S - (it + 1) * span, span)
        af = af_ref[pl.ds(f0, span), :]
        uf = uf_ref[pl.ds(f0, span), :]
        hs = []
        for j in range(unroll):
            a, b = _scan8(af[8 * j:8 * j + 8], uf[8 * j:8 * j + 8], False)
            hs.append(a * cf + b)
            cf = jnp.broadcast_to(a[7:8], (8, C)) * cf + jnp.broadcast_to(b[7:8], (8, C))
        uf_ref[pl.ds(f0, span), :] = jnp.concatenate(hs, axis=0)
        ab = ab_ref[pl.ds(b0, span), :]
        ub = ub_ref[pl.ds(b0, span), :]
        hs = []
        for j in reversed(range(unroll)):
            a, b = _scan8(ab[8 * j:8 * j + 8], ub[8 * j:8 * j + 8], True)
            hs.append(a * cb + b)
            cb = jnp.broadcast_to(a[0:1], (8, C)) * cb + jnp.broadcast_to(b[0:1], (8, C))
        ub_ref[pl.ds(b0, span), :] = jnp.concatenate(hs[::-1], axis=0)
        return cf, cb

    zero = jnp.zeros((8, C), F32)
    lax.fori_loop(0, S // span, body, (zero, zero))

    for c in range(S // tc):
        rows = slice(c * tc, (c + 1) * tc)
        y = yg_ref[0, rows, :]
        half_y = 0.5 * y
        gelu = half_y * jnp.tanh(y * ((y * y) * (GELU_C * GELU_CUBIC) + GELU_C)) + half_y
        o_ref[0, rows, :] = ((uf_ref[rows, :] + ub_ref[rows, :]) * gelu).astype(o_ref.dtype)


def _lru(xr, yg, conv_w, conv_b, w_a, b_a, w_i, b_i, lam):
    B, S, W = xr.shape
    C = LRU_HEAD_DIM
    H = W // C
    per_head = lambda a: a.reshape(H, 1, C)
    wg = 0.5 * jnp.concatenate([w_a[0], w_i[0], w_a[1], w_i[1]], axis=-1)
    bg = 0.5 * jnp.concatenate([per_head(b_a[0]), per_head(b_i[0]), per_head(b_a[1]), per_head(b_i[1])], axis=-1)
    bg_hi = bg.astype(BF16)
    bg_lo = (bg - bg_hi.astype(F32)).astype(BF16)
    wg = jnp.concatenate([wg.astype(BF16), bg_hi, bg_lo, jnp.zeros((H, C - 2, 4 * C), BF16)], axis=1)
    lam2 = jnp.concatenate([per_head(lam[0]), per_head(lam[1])], axis=-1)
    seq = pl.BlockSpec((1, S, C), lambda b, h: (b, 0, h))
    return pl.pallas_call(
        _lru_kernel,
        grid=(B, H),
        in_specs=[
            seq, seq,
            pl.BlockSpec((CONV_WIDTH, C), lambda b, h: (0, h)),
            pl.BlockSpec((1, C), lambda b, h: (0, h)),
            pl.BlockSpec((1, 2 * C, 4 * C), lambda b, h: (h, 0, 0)),
            pl.BlockSpec((1, 1, 2 * C), lambda b, h: (h, 0, 0)),
        ],
        out_specs=seq,
        out_shape=jax.ShapeDtypeStruct((B, S, W), BF16),
        scratch_shapes=[pltpu.VMEM((S + 16, C), F32)] + [pltpu.VMEM((S, C), F32) for _ in range(4)],
        compiler_params=_params("parallel", "parallel"),
        name="rg_lru",
    )(xr, yg, conv_w, conv_b.reshape(1, W), wg, lam2)


@functools.lru_cache(maxsize=None)
def _dft_tables(seq, group_dim):
    n4 = seq // FOURIER_RADIX
    k1 = np.arange(n4, dtype=np.int64)[:, None]
    s1 = np.arange(n4, dtype=np.int64)[None, :]
    cos_t, sin_t = [], []
    for s2 in range(FOURIER_RADIX):
        ang = 2.0 * np.pi * ((k1 * (FOURIER_RADIX * s1 + s2)) % seq).astype(np.float64) / seq
        cos_t.append(np.cos(ang) / np.sqrt(seq))
        sin_t.append(-np.sin(ang) / np.sqrt(seq))
    c = np.arange(group_dim, dtype=np.int64)
    ang = 2.0 * np.pi * ((c[:, None] * c[None, :]) % group_dim).astype(np.float64) / group_dim
    chan = np.concatenate([np.cos(ang), np.sin(ang)], axis=0) / np.sqrt(group_dim)
    return (np.stack(cos_t).astype(np.float32), np.stack(sin_t).astype(np.float32), chan.astype(np.float32))


def _fourier_kernel(h0_ref, h1_ref, h2_ref, h3_ref, fc_ref, fs_ref, cs_ref, o_ref):
    n4 = fc_ref.shape[1]
    rows = 512
    nkc = n4 // rows

    def sequence_dft(kc):
        rs = slice(kc * rows, (kc + 1) * rows)
        yr, yi = [], []
        for r, h_ref in enumerate((h0_ref, h1_ref, h2_ref, h3_ref)):
            x = h_ref[0, 0]
            yr.append(jnp.dot(fc_ref[r, rs, :], x, preferred_element_type=F32))
            yi.append(jnp.dot(fs_ref[r, rs, :], x, preferred_element_type=F32))
        xs = (
            (yr[0] + yr[1] + yr[2] + yr[3], yi[0] + yi[1] + yi[2] + yi[3]),
            (yr[0] + yi[1] - yr[2] - yi[3], yi[0] - yr[1] - yi[2] + yr[3]),
            (yr[0] - yr[1] + yr[2] - yr[3], yi[0] - yi[1] + yi[2] - yi[3]),
            (yr[0] - yi[1] - yr[2] + yi[3], yi[0] + yr[1] - yi[2] - yr[3]),
        )
        return [jnp.concatenate([xr.astype(BF16), xi.astype(BF16)], axis=1) for xr, xi in xs]

    def channel_dft(kc, zs):
        for k2, z in enumerate(zs):
            f = jnp.dot(z, cs_ref[...], preferred_element_type=F32)
            o_ref[0, k2 * n4 + kc * rows:k2 * n4 + (kc + 1) * rows, :] = f.astype(o_ref.dtype)

    pending = None
    for kc in range(nkc):
        zs = sequence_dft(kc)
        if pending is not None:
            channel_dft(kc - 1, pending)
        pending = zs
    channel_dft(nkc - 1, pending)


def _fourier(h4):
    B, _, n4, D = h4.shape
    S = n4 * FOURIER_RADIX
    G = D // FOURIER_GROUPS
    fc, fs, cs = (jnp.asarray(t).astype(BF16) for t in _dft_tables(S, G))
    per_res = lambda r: pl.BlockSpec((1, 1, n4, G), lambda b, g: (b, r, 0, g))
    return pl.pallas_call(
        _fourier_kernel,
        grid=(B, FOURIER_GROUPS),
        in_specs=[per_res(0), per_res(1), per_res(2), per_res(3),
                  _resident(fc.shape), _resident(fs.shape), _resident(cs.shape)],
        out_specs=pl.BlockSpec((1, S, G), lambda b, g: (b, 0, g)),
        out_shape=jax.ShapeDtypeStruct((B, S, D), BF16),
        compiler_params=_params("parallel", "parallel"),
        name="fourier",
    )(h4, h4, h4, h4, fc, fs, cs)


FFN_ROW_SPLITS = 2
FFN_COL_CHUNK = 256


def _post_ffn_kernel(emit_next, x_ref, a_ref, b_ref, wa_ref, wb_ref, gpost_ref, gpre_ref, wg_ref, wu_ref,
                     wd_ref, gfpost_ref, gnext_ref, pick_ref, o_ref, *rest):
    act_ref = rest[-1]
    tm = x_ref.shape[0]
    dff = wg_ref.shape[1]
    groups = [slice(s * (tm // FFN_ROW_SPLITS), (s + 1) * (tm // FFN_ROW_SPLITS)) for s in range(FFN_ROW_SPLITS)]

    def mix(rows):
        return (jnp.dot(a_ref[rows, :], wa_ref[...], preferred_element_type=F32)
                + jnp.dot(b_ref[rows, :], wb_ref[...], preferred_element_type=F32))

    def gate_up(rows, h):
        for c in range(dff // FFN_COL_CHUNK):
            cols = slice(c * FFN_COL_CHUNK, (c + 1) * FFN_COL_CHUNK)
            g = jnp.dot(h, wg_ref[:, cols], preferred_element_type=F32)
            u = jnp.dot(h, wu_ref[:, cols], preferred_element_type=F32)
            half_g = 0.5 * g
            act_ref[rows, cols] = ((half_g * jnp.tanh(half_g) + half_g) * u).astype(BF16)

    mixed = [mix(rows) for rows in groups]
    xs = []
    for rows, m in zip(groups, mixed):
        x = x_ref[rows, :] + _rms(m, gpost_ref[...])
        xs.append(x)
        gate_up(rows, _rms(x, gpre_ref[...]).astype(BF16))
    normed = []
    for rows, x in zip(groups, xs):
        y = jnp.dot(act_ref[rows, :], wd_ref[...], preferred_element_type=F32)
        x = x + _rms(y, gfpost_ref[...])
        o_ref[rows, :] = x
        if emit_next:
            normed.append(_rms(x, gnext_ref[...]).astype(BF16))
    for s, hn in enumerate(normed):
        hn_ref = rest[0]
        n = hn.shape[0] // FOURIER_RADIX
        picked = jnp.dot(pick_ref[...], hn, preferred_element_type=F32).astype(hn_ref.dtype)
        for r in range(FOURIER_RADIX):
            hn_ref[0, r, s * n:(s + 1) * n, :] = picked[r * n:(r + 1) * n]


def _post_ffn(x, a, a_col, b, b_col, w_mix, g_post, g_pre, wg, wu, wd, g_fpost, g_next, seq, tm=512):
    T, D = x.shape
    half = w_mix.shape[0] // 2
    dff = wg.shape[1]
    emit_next = g_next is not None
    rows_per_group = tm // FFN_ROW_SPLITS
    n_pick = rows_per_group // FOURIER_RADIX
    pick = np.zeros((FOURIER_RADIX, n_pick, rows_per_group), np.float32)
    for r in range(FOURIER_RADIX):
        pick[r, np.arange(n_pick), FOURIER_RADIX * np.arange(n_pick) + r] = 1.0
    pick = jnp.asarray(pick.reshape(rows_per_group, rows_per_group), BF16)
    row = lambda i: (i, 0)
    vec = lambda g: g.reshape(1, D)
    out_specs = [pl.BlockSpec((tm, D), row)]
    out_shape = [jax.ShapeDtypeStruct((T, D), F32)]
    if emit_next:
        steps = seq // tm
        out_specs.append(pl.BlockSpec((1, FOURIER_RADIX, tm // FOURIER_RADIX, D),
                                      lambda i: (i // steps, 0, i % steps, 0)))
        out_shape.append(jax.ShapeDtypeStruct((T // seq, FOURIER_RADIX, seq // FOURIER_RADIX, D), BF16))
    outs = pl.pallas_call(
        functools.partial(_post_ffn_kernel, emit_next),
        grid=(T // tm,),
        in_specs=[
            pl.BlockSpec((tm, D), row),
            pl.BlockSpec((tm, half), lambda i: (i, a_col)),
            pl.BlockSpec((tm, half), lambda i: (i, b_col)),
            pl.BlockSpec((half, D), lambda i: (0, 0), pipeline_mode=pl.Buffered(1)),
            pl.BlockSpec((half, D), lambda i: (1, 0), pipeline_mode=pl.Buffered(1)),
            _resident((1, D)), _resident((1, D)),
            _resident(wg.shape), _resident(wu.shape), _resident(wd.shape),
            _resident((1, D)), _resident((1, D)), _resident(pick.shape),
        ],
        out_specs=out_specs,
        out_shape=out_shape,
        scratch_shapes=[pltpu.VMEM((tm, dff), BF16)],
        compiler_params=_params("parallel"),
        name="post_ffn",
    )(x, a, b, w_mix, w_mix, vec(g_post), vec(g_pre), wg, wu, wd, vec(g_fpost),
      vec(g_next if emit_next else g_fpost), pick)
    return (outs[0], outs[1]) if emit_next else (outs[0], None)


def kernel(x, ln_mix_pre, ln_mix_post, ln_ffn_pre, ln_ffn_post, w_in, w_mix_out, lambda_q1, lambda_k1,
           lambda_q2, lambda_k2, attn_subln, conv_w, conv_b, lru_w_a, lru_b_a, lru_w_i, lru_b_i, lru_lambda,
           w_fourier_out, w_ffn_gate, w_ffn_up, w_ffn_down):
    B, S, D = x.shape
    T = B * S
    depth = ln_mix_pre.shape[0]
    heads = jnp.arange(1, ATTN_HEADS + 1, dtype=F32)
    slopes = jnp.broadcast_to(jnp.exp2(-8.0 * heads / ATTN_HEADS)[:, None, None], (ATTN_HEADS, 1, HEAD_W))
    w_in, w_mix_out, w_fourier_out, w_ffn_gate, w_ffn_up, w_ffn_down = (
        w.astype(BF16) for w in (w_in, w_mix_out, w_fourier_out, w_ffn_gate, w_ffn_up, w_ffn_down))
    xf = x.reshape(T, D)
    h4 = None
    for l in range(depth):
        if l % 2 == 0:
            e = l // 2
            lam_init = 0.8 - 0.6 * float(np.exp(-0.3 * l))
            q, k, v, xr, yg = _inproj(xf, ln_mix_pre[l], w_in[e])
            width = q.shape[1]
            to_seq = lambda a: a.reshape(B, S, width)
            attn = _attention(to_seq(q), to_seq(k), to_seq(v), slopes, lambda_q1[e], lambda_k1[e],
                              lambda_q2[e], lambda_k2[e], attn_subln[e], lam_init)
            rec = _lru(to_seq(xr), to_seq(yg), conv_w[e], conv_b[e], lru_w_a[e], lru_b_a[e],
                       lru_w_i[e], lru_b_i[e], lru_lambda[e])
            a, a_col = attn.reshape(T, width), 0
            b, b_col = rec.reshape(T, width), 0
            w_mix = w_mix_out[e]
        else:
            f = _fourier(h4).reshape(T, D)
            a, a_col, b, b_col = f, 0, f, 1
            w_mix = w_fourier_out[l // 2]
        next_is_odd = l + 1 < depth and (l + 1) % 2 == 1
        xf, h4 = _post_ffn(xf, a, a_col, b, b_col, w_mix, ln_mix_post[l], ln_ffn_pre[l],
                           w_ffn_gate[l], w_ffn_up[l], w_ffn_down[l],
                           ln_ffn_post[l], ln_mix_pre[l + 1] if next_is_odd else None, S)
    return xf.reshape(B, S, D)
```
